```python
import math
import jax, jax.numpy as jnp
from jax import lax
import numpy as np

D_MODEL = 2048
BATCH = 1
SEQ = 8192
DEPTH = 1
DEC_BATCH = 16
DEC_SEQ = 2048
PAST_LEN = 128

PLE_DIM = 256
A_WIDTH = 1024
A_GROUPS = 8
A_GROUP_DIM = A_WIDTH // A_GROUPS
A_CHUNK = 128
B_HEADS = 8
B_HEAD_DIM = 128
B_WIDTH = B_HEADS * B_HEAD_DIM
B_CONV = 3
B_CHUNK = 64
Q_SCALE = 1.0 / math.sqrt(B_HEAD_DIM)
D_FF = 4096
FFN_CONV = 3
EPS = 1e-6
IN_SPLITS = (A_WIDTH, A_WIDTH, 3 * B_WIDTH, B_WIDTH, B_HEADS, B_HEADS, B_HEADS, B_HEADS, D_MODEL, D_MODEL)
N_IN = 2 * A_WIDTH + 4 * B_WIDTH + 4 * B_HEADS + 2 * D_MODEL

kernel_name = 'hybrid_gmlp_gdn_bidir_encoder'


def rmsnorm(x, g):
    xf = x.astype(jnp.float32)
    y = xf * lax.rsqrt(jnp.mean(xf * xf, axis=-1, keepdims=True) + EPS)
    return (y * g.astype(jnp.float32)).astype(x.dtype)


def l2norm(x):
    return x * lax.rsqrt(jnp.sum(x * x, axis=-1, keepdims=True) + EPS)


def split_cols(t, sizes):
    out, start = [], 0
    for s in sizes:
        out.append(t[..., start:start + s])
        start += s
    return out


def dwconv_centred(x, w):
    k, c = w.shape
    return lax.conv_general_dilated(
        x, w[:, None, :].astype(x.dtype), window_strides=(1,),
        padding=[(k // 2, k // 2)], dimension_numbers=('NWC', 'WIO', 'NWC'),
        feature_group_count=c)


def spatial_gating_unit(u, v, g_v, w_s, b_s):
    bsz, seq, _ = v.shape
    v = rmsnorm(v, g_v).reshape(bsz, seq // A_CHUNK, A_CHUNK, A_GROUPS, A_GROUP_DIM)
    mixed = jnp.einsum('gij,bnjgc->bnigc', w_s, v) + b_s.T[:, :, None]
    return u * mixed.reshape(bsz, seq, A_WIDTH)


def gated_delta_rule(q, k, v, g, beta):
    bsz, seq, nh, dk = q.shape
    dv = v.shape[-1]
    c = B_CHUNK
    n = seq // c

    def chunks(t):
        return jnp.moveaxis(t.reshape((bsz, n, c, nh) + t.shape[3:]), 3, 1)

    q, k, v, g, beta = chunks(q), chunks(k), chunks(v), chunks(g), chunks(beta)
    g = jnp.cumsum(g, axis=-1)
    tril = jnp.tril(jnp.ones((c, c), dtype=bool))
    strict = jnp.tril(jnp.ones((c, c), dtype=bool), -1)
    decay = jnp.exp(jnp.where(tril, g[..., :, None] - g[..., None, :], -jnp.inf))
    k_beta = k * beta[..., None]
    v_beta = v * beta[..., None]
    lower = jnp.where(strict, jnp.einsum('bhnid,bhnjd->bhnij', k_beta, k) * decay, 0.0)
    eye = jnp.eye(c, dtype=lower.dtype)
    a_mat = lower + eye
    t_inv = lax.linalg.triangular_solve(a_mat, jnp.broadcast_to(eye, a_mat.shape),
                                        left_side=True, lower=True, unit_diagonal=True)
    u = jnp.einsum('bhnij,bhnjd->bhnid', t_inv, v_beta)
    w = jnp.einsum('bhnij,bhnjd->bhnid', t_inv, k_beta * jnp.exp(g)[..., None])
    qk = jnp.where(tril, jnp.einsum('bhnid,bhnjd->bhnij', q, k) * decay, 0.0)
    g_last = g[..., -1]
    q_dec = q * jnp.exp(g)[..., None]
    k_dec = k * jnp.exp(g_last[..., None] - g)[..., None]

    def step(state, xs):
        qk_n, u_n, w_n, qd_n, kd_n, gl_n = xs
        v_new = u_n - jnp.einsum('bhcd,bhde->bhce', w_n, state)
        o = jnp.einsum('bhcd,bhde->bhce', qd_n, state) + jnp.einsum('bhij,bhje->bhie', qk_n, v_new)
        state = state * jnp.exp(gl_n)[..., None, None] + jnp.einsum('bhcd,bhce->bhde', kd_n, v_new)
        return state, o

    xs = tuple(jnp.moveaxis(t, 2, 0) for t in (qk, u, w, q_dec, k_dec, g_last))
    state0 = jnp.zeros((bsz, nh, dk, dv), jnp.float32)
    _, o = lax.scan(step, state0, xs)
    o = jnp.moveaxis(jnp.moveaxis(o, 0, 2), 1, 3)
    return o.reshape(bsz, seq, nh, dv)


def gated_deltanet_mixer(qkv, z, beta_f, beta_b, alpha_f, alpha_b, conv_qkv,
                         a_log_f, a_log_b, dt_bias_f, dt_bias_b, g_b_out):
    bsz, seq, _ = qkv.shape
    f32 = jnp.float32
    qkv = jax.nn.silu(dwconv_centred(qkv, conv_qkv)).astype(f32)
    q, k, v = split_cols(qkv, (B_WIDTH, B_WIDTH, B_WIDTH))
    q = l2norm(q.reshape(bsz, seq, B_HEADS, B_HEAD_DIM)) * Q_SCALE
    k = l2norm(k.reshape(bsz, seq, B_HEADS, B_HEAD_DIM))
    v = v.reshape(bsz, seq, B_HEADS, B_HEAD_DIM)

    def log_decay(alpha, a_log, dt_bias):
        return -jnp.exp(a_log.astype(f32)) * jax.nn.softplus(alpha.astype(f32) + dt_bias.astype(f32))

    g_f = log_decay(alpha_f, a_log_f, dt_bias_f)
    g_b = log_decay(alpha_b, a_log_b, dt_bias_b)
    bt_f = jax.nn.sigmoid(beta_f.astype(f32))
    bt_b = jax.nn.sigmoid(beta_b.astype(f32))
    flip = lambda t: jnp.flip(t, axis=1)
    o_fwd = gated_delta_rule(q, k, v, g_f, bt_f)
    o_bwd = flip(gated_delta_rule(flip(q), flip(k), flip(v), flip(g_b), flip(bt_b)))
    o = rmsnorm(o_fwd + o_bwd, g_b_out) * jax.nn.silu(z.astype(f32).reshape(bsz, seq, B_HEADS, B_HEAD_DIM))
    return o.reshape(bsz, seq, B_WIDTH).astype(z.dtype)


def encoder_layer(x, p, g_mix, w_in, g_a_v, w_s, b_s, conv_qkv, a_log_f, a_log_b,
                  dt_bias_f, dt_bias_b, g_b_out, w_a_out, w_b_out, w_out, g_ffn, w_up,
                  conv_ffn, w_down, g_ple, w_ple_gate, w_ple):
    h = rmsnorm(x, g_mix)
    proj = h @ w_in
    (u_a, v_a, qkv, z, beta_f, beta_b, alpha_f, alpha_b,
     gate_a, gate_b) = split_cols(proj, IN_SPLITS)
    a = spatial_gating_unit(jax.nn.gelu(u_a, approximate=False),
                            jax.nn.gelu(v_a, approximate=False), g_a_v, w_s, b_s)
    b = gated_deltanet_mixer(qkv, z, beta_f, beta_b, alpha_f, alpha_b, conv_qkv,
                             a_log_f, a_log_b, dt_bias_f, dt_bias_b, g_b_out)
    merged = jax.nn.sigmoid(gate_a) * (a @ w_a_out) + jax.nn.sigmoid(gate_b) * (b @ w_b_out)
    x = x + merged @ w_out
    h = rmsnorm(x, g_ffn)
    up = dwconv_centred(h @ w_up, conv_ffn)
    x = x + (jax.nn.silu(up[..., :D_FF]) * up[..., D_FF:]) @ w_down
    ple_gate = jax.nn.sigmoid(rmsnorm(x, g_ple) @ w_ple_gate)
    return x + ple_gate * (p @ w_ple)


def setup_inputs(seed: int = 0) -> dict:
    key = jax.random.key(seed)
    keys = iter(jax.random.split(key, 40))
    f32 = jnp.float32

    def nrm(shape, scale):
        return jax.random.normal(next(keys), shape, f32) * scale

    def gain(shape):
        return 1.0 + 0.02 * jax.random.normal(next(keys), shape, f32)

    def a_log():
        return jnp.log(jax.random.uniform(next(keys), (DEPTH, B_HEADS), f32, 1.0, 16.0))

    def dt_bias():
        dt = jnp.exp(jax.random.uniform(next(keys), (DEPTH, B_HEADS), f32,
                                        math.log(1e-3), math.log(1e-1)))
        return dt + jnp.log(-jnp.expm1(-dt))

    return {
        'x_prompt': nrm((BATCH, SEQ, D_MODEL), 1.0),
        'x_sample': nrm((DEC_BATCH, DEC_SEQ, D_MODEL), 1.0),
        'p_prompt': nrm((DEPTH, BATCH, SEQ, PLE_DIM), 1.0),
        'p_sample': nrm((DEPTH, DEC_BATCH, DEC_SEQ, PLE_DIM), 1.0),
        'g_mix': gain((DEPTH, D_MODEL)),
        'w_in': nrm((DEPTH, D_MODEL, N_IN), D_MODEL ** -0.5),
        'g_a_v': gain((DEPTH, A_WIDTH)),
        'w_s': nrm((DEPTH, A_GROUPS, A_CHUNK, A_CHUNK), A_CHUNK ** -0.5),
        'b_s': gain((DEPTH, A_GROUPS, A_CHUNK)),
        'conv_qkv': nrm((DEPTH, B_CONV, 3 * B_WIDTH), B_CONV ** -0.5),
        'a_log_f': a_log(),
        'a_log_b': a_log(),
        'dt_bias_f': dt_bias(),
        'dt_bias_b': dt_bias(),
        'g_b_out': gain((DEPTH, B_HEAD_DIM)),
        'w_a_out': nrm((DEPTH, A_WIDTH, D_MODEL), A_WIDTH ** -0.5),
        'w_b_out': nrm((DEPTH, B_WIDTH, D_MODEL), B_WIDTH ** -0.5),
        'w_out': nrm((DEPTH, D_MODEL, D_MODEL), D_MODEL ** -0.5),
        'g_ffn': gain((DEPTH, D_MODEL)),
        'w_up': nrm((DEPTH, D_MODEL, 2 * D_FF), D_MODEL ** -0.5),
        'conv_ffn': nrm((DEPTH, FFN_CONV, 2 * D_FF), FFN_CONV ** -0.5),
        'w_down': nrm((DEPTH, D_FF, D_MODEL), D_FF ** -0.5),
        'g_ple': gain((DEPTH, D_MODEL)),
        'w_ple_gate': nrm((DEPTH, D_MODEL, D_MODEL), D_MODEL ** -0.5),
        'w_ple': nrm((DEPTH, PLE_DIM, D_MODEL), PLE_DIM ** -0.5),
        'g_final': gain((D_MODEL,)),
    }


def reference(x_prompt, x_sample, p_prompt, p_sample, g_mix, w_in, g_a_v, w_s, b_s,
              conv_qkv, a_log_f, a_log_b, dt_bias_f, dt_bias_b, g_b_out, w_a_out,
              w_b_out, w_out, g_ffn, w_up, conv_ffn, w_down, g_ple, w_ple_gate, w_ple,
              g_final):
    def trunk(x, p):
        for i in range(DEPTH):
            x = encoder_layer(x, p[i], g_mix[i], w_in[i], g_a_v[i], w_s[i], b_s[i],
                              conv_qkv[i], a_log_f[i], a_log_b[i], dt_bias_f[i],
                              dt_bias_b[i], g_b_out[i], w_a_out[i], w_b_out[i], w_out[i],
                              g_ffn[i], w_up[i], conv_ffn[i], w_down[i], g_ple[i],
                              w_ple_gate[i], w_ple[i])
        return rmsnorm(x, g_final)

    y_prompt = trunk(x_prompt, p_prompt)
    y_sample = trunk(x_sample, p_sample)
    return (y_prompt, y_sample)
```

```python
import functools
import math

import jax
import jax.numpy as jnp
from jax import lax
from jax.experimental import pallas as pl
from jax.experimental.pallas import tpu as pltpu

F32 = jnp.float32
BF16 = jnp.bfloat16

D_MODEL = 2048
PLE_DIM = 256
A_WIDTH = 1024
A_GROUPS = 8
A_CHUNK = 128
B_HEADS = 8
B_HEAD_DIM = 128
B_WIDTH = B_HEADS * B_HEAD_DIM
B_CHUNK = 64
Q_SCALE = 1.0 / math.sqrt(B_HEAD_DIM)
D_FF = 4096
EPS = 1e-6

N_MAIN = 2 * A_WIDTH + 4 * B_WIDTH + 2 * D_MODEL
N_SMALL = 4 * B_HEADS

TOKEN_TILE = 512
PROJ_COL_TILE = 1024
FFN_COL_TILE = 512
HALO = 16
GDN_SLAB = 256
VMEM_LIMIT = 60000 * 1024


def _rms(x, g):
    return x * lax.rsqrt(jnp.mean(x * x, axis=-1, keepdims=True) + EPS) * g


def _gelu(x):
    return 0.5 * x * (1.0 + lax.erf(x * math.sqrt(0.5)))


def _dot(a, b):
    return jnp.dot(a, b, preferred_element_type=F32)


def _dot_nt(a, b):
    return lax.dot_general(a, b, (((1,), (1,)), ((), ())), preferred_element_type=F32)


def _dot_tn(a, b):
    return lax.dot_general(a, b, (((0,), (0,)), ((), ())), preferred_element_type=F32)


def _split3(x):
    hi = x.astype(BF16)
    r1 = x - hi.astype(F32)
    mid = r1.astype(BF16)
    lo = (r1 - mid.astype(F32)).astype(BF16)
    return hi, mid, lo


def _mm_split(a, b):
    ah = a.astype(BF16)
    al = (a - ah.astype(F32)).astype(BF16)
    bh = b.astype(BF16)
    bl = (b - bh.astype(F32)).astype(BF16)
    lhs = jnp.concatenate([ah, ah, al], axis=1)
    rhs = jnp.concatenate([bh, bl, bh], axis=0)
    return _dot(lhs, rhs)


def _in_proj_kernel(x_ref, gmix_ref, w_ref, wsm_ref, wsmt_ref, gav_ref,
                    main_ref, small_ref, smallt_ref, h_ref):
    j = pl.program_id(1)

    @pl.when(j == 0)
    def _():
        h = _rms(x_ref[...], gmix_ref[...]).astype(BF16)
        h_ref[...] = h
        small_ref[...] = _dot(h, wsm_ref[...])
        smallt_ref[...] = _dot_nt(wsmt_ref[...], h)

    acc = _dot(h_ref[...], w_ref[...])

    @pl.when(j == 0)
    def _():
        main_ref[...] = _gelu(acc).astype(BF16)

    @pl.when(j == 1)
    def _():
        v = _gelu(acc)
        main_ref[...] = _rms(v, gav_ref[...]).astype(BF16)

    @pl.when((j >= 2) & (j <= 4))
    def _():
        main_ref[...] = acc.astype(BF16)

    @pl.when(j == 5)
    def _():
        main_ref[...] = (acc * jax.nn.sigmoid(acc)).astype(BF16)

    @pl.when(j >= 6)
    def _():
        main_ref[...] = jax.nn.sigmoid(acc).astype(BF16)


def _in_proj(x, gmix, w_main, w_small, w_small_t, gav):
    t = x.shape[0]
    tm, tn = TOKEN_TILE, PROJ_COL_TILE
    return pl.pallas_call(
        _in_proj_kernel,
        out_shape=(jax.ShapeDtypeStruct((t, N_MAIN), BF16),
                   jax.ShapeDtypeStruct((t, 128), F32),
                   jax.ShapeDtypeStruct((2 * B_HEADS, t), F32)),
        grid=(t // tm, N_MAIN // tn),
        in_specs=[
            pl.BlockSpec((tm, D_MODEL), lambda i, j: (i, 0)),
            pl.BlockSpec((1, D_MODEL), lambda i, j: (0, 0)),
            pl.BlockSpec((D_MODEL, tn), lambda i, j: (0, j)),
            pl.BlockSpec((D_MODEL, 128), lambda i, j: (0, 0)),
            pl.BlockSpec((2 * B_HEADS, D_MODEL), lambda i, j: (0, 0)),
            pl.BlockSpec((1, A_WIDTH), lambda i, j: (0, 0)),
        ],
        out_specs=(
            pl.BlockSpec((tm, tn), lambda i, j: (i, j)),
            pl.BlockSpec((tm, 128), lambda i, j: (i, 0)),
            pl.BlockSpec((2 * B_HEADS, tm), lambda i, j: (0, i)),
        ),
        scratch_shapes=[pltpu.VMEM((tm, D_MODEL), BF16)],
        compiler_params=pltpu.CompilerParams(
            dimension_semantics=("parallel", "arbitrary"), vmem_limit_bytes=VMEM_LIMIT),
        name="in_proj",
    )(x, gmix, w_main, w_small, w_small_t, gav)


def _softplus(x):
    return jnp.maximum(x, 0.0) + jnp.log1p(jnp.exp(-jnp.abs(x)))


def _gate_prep_kernel(small_ref, smallt_ref, alog_r, dtb_r, alog_c, dtb_c, gbh_ref, grh_ref):
    tb = small_ref.shape[0]
    nl = N_SMALL
    row = lax.broadcasted_iota(jnp.int32, (tb, tb), 0)
    col = lax.broadcasted_iota(jnp.int32, (tb, tb), 1)
    same = (row // B_CHUNK) == (col // B_CHUNK)
    m_lo = jnp.where(same & (col <= row), 1.0, 0.0).astype(BF16)
    m_up = jnp.where(same & (col >= row), 1.0, 0.0).astype(BF16)

    sm = small_ref[:, 0:nl]
    g = -jnp.exp(alog_r[...]) * _softplus(sm + dtb_r[...])
    g3 = jnp.concatenate(_split3(g), axis=1)
    pf = _dot(m_lo, g3)
    pb = _dot(m_up, g3)
    pf = pf[:, 0:nl] + pf[:, nl:2 * nl] + pf[:, 2 * nl:3 * nl]
    pb = pb[:, 0:nl] + pb[:, nl:2 * nl] + pb[:, 2 * nl:3 * nl]
    lane = lax.broadcasted_iota(jnp.int32, (tb, nl), 1) % 4
    vals = jnp.where(lane == 0, pf, jnp.where(lane == 1, pb, jax.nn.sigmoid(sm)))
    for h in range(B_HEADS):
        gbh_ref[h] = vals[:, 4 * h:4 * h + 4]

    gt = -jnp.exp(alog_c[...]) * _softplus(smallt_ref[...] + dtb_c[...])
    gt3 = jnp.concatenate(_split3(gt), axis=0)
    nr = 2 * B_HEADS
    rf = _dot(gt3, m_up)
    rb = _dot(gt3, m_lo)
    rf = rf[0:nr] + rf[nr:2 * nr] + rf[2 * nr:3 * nr]
    rb = rb[0:nr] + rb[nr:2 * nr] + rb[2 * nr:3 * nr]
    sub = lax.broadcasted_iota(jnp.int32, (nr, tb), 0) % 2
    grow = jnp.where(sub == 0, rf, rb)
    for c in range(tb // B_CHUNK):
        for h in range(B_HEADS):
            grh_ref[h, c] = grow[2 * h:2 * h + 2, c * B_CHUNK:(c + 1) * B_CHUNK]


def _gate_prep(small, small_t, alog_r, dtb_r, alog_c, dtb_c):
    t = small.shape[0]
    tb = TOKEN_TILE
    nc = tb // B_CHUNK
    return pl.pallas_call(
        _gate_prep_kernel,
        out_shape=(jax.ShapeDtypeStruct((B_HEADS, t, 4), F32),
                   jax.ShapeDtypeStruct((B_HEADS, t // B_CHUNK, 2, B_CHUNK), F32)),
        grid=(t // tb,),
        in_specs=[
            pl.BlockSpec((tb, 128), lambda i: (i, 0)),
            pl.BlockSpec((2 * B_HEADS, tb), lambda i: (0, i)),
            pl.BlockSpec((1, N_SMALL), lambda i: (0, 0)),
            pl.BlockSpec((1, N_SMALL), lambda i: (0, 0)),
            pl.BlockSpec((2 * B_HEADS, 1), lambda i: (0, 0)),
            pl.BlockSpec((2 * B_HEADS, 1), lambda i: (0, 0)),
        ],
        out_specs=(
            pl.BlockSpec((B_HEADS, tb, 4), lambda i: (0, i, 0)),
            pl.BlockSpec((B_HEADS, nc, 2, B_CHUNK), lambda i: (0, i, 0, 0)),
        ),
        compiler_params=pltpu.CompilerParams(
            dimension_semantics=("parallel",), vmem_limit_bytes=VMEM_LIMIT),
        name="gate_prep",
    )(small, small_t, alog_r, dtb_r, alog_c, dtb_c)


def _inv_unit_tri(l_mat):
    c = l_mat.shape[0]
    n = -l_mat
    row = lax.broadcasted_iota(jnp.int32, (c, c), 0)
    col = lax.broadcasted_iota(jnp.int32, (c, c), 1)
    x = jnp.where(row == col, 1.0, 0.0) + n
    p = _mm_split(n, n)
    levels = int(math.log2(c))
    for lvl in range(1, levels):
        if lvl < levels - 1:
            px = _mm_split(p, jnp.concatenate([p, x], axis=1))
            x = x + px[:, c:]
            p = px[:, :c]
        else:
            x = x + _mm_split(p, x)
    return x


def _gdn_kernel(q_ref, k_ref, v_ref, cq_ref, ck_ref, cv_ref, gbh_ref, grh_ref, z_ref, gout_ref,
                o_ref, qs, ks, vs, oacc, *, seq_len):
    s = seq_len
    c = B_CHUNK
    n_chunks = s // c
    sl = min(GDN_SLAB, s)
    rows_sl = lax.broadcasted_iota(jnp.int32, (sl, B_HEAD_DIM), 0)

    def conv_silu(x_ref, c_ref, r0):
        xm = x_ref[pl.ds(r0, sl), :].astype(F32)
        p0 = pl.multiple_of(jnp.maximum(r0 - HALO, 0), HALO)
        n0 = pl.multiple_of(jnp.minimum(r0 + sl, s - HALO), HALO)
        prev = x_ref[pl.ds(p0, HALO), :].astype(F32)[HALO - 1:HALO, :]
        nxt = x_ref[pl.ds(n0, HALO), :].astype(F32)[0:1, :]
        prev = jnp.where(r0 > 0, prev, 0.0)
        nxt = jnp.where(r0 + sl < s, nxt, 0.0)
        xp = jnp.where(rows_sl == 0, prev, pltpu.roll(xm, 1, 0))
        xn = jnp.where(rows_sl == sl - 1, nxt, pltpu.roll(xm, sl - 1, 0))
        w = c_ref[...]
        y = w[0:1, :] * xp + w[1:2, :] * xm + w[2:3, :] * xn
        return y * jax.nn.sigmoid(y)

    def l2n(y):
        return y * lax.rsqrt(jnp.sum(y * y, axis=-1, keepdims=True) + EPS)

    def pre(i, carry):
        r0 = pl.multiple_of(i * sl, sl)
        qs[pl.ds(r0, sl), :] = (l2n(conv_silu(q_ref, cq_ref, r0)) * Q_SCALE).astype(BF16)
        ks[pl.ds(r0, sl), :] = l2n(conv_silu(k_ref, ck_ref, r0)).astype(BF16)
        vs[pl.ds(r0, sl), :] = conv_silu(v_ref, cv_ref, r0).astype(BF16)
        oacc[pl.ds(r0, sl), :] = jnp.zeros((sl, B_HEAD_DIM), F32)
        return carry

    lax.fori_loop(0, s // sl, pre, 0)

    rowi = lax.broadcasted_iota(jnp.int32, (c, c), 0)
    coli = lax.broadcasted_iota(jnp.int32, (c, c), 1)

    def chunk(ci, state, fwd):
        r0 = pl.multiple_of(ci * c, c)
        q = qs[pl.ds(r0, c), :]
        k = ks[pl.ds(r0, c), :]
        v = vs[pl.ds(r0, c), :]
        gb = gbh_ref[pl.ds(r0, c), :]
        gr = grh_ref[ci]
        if fwd:
            gi, beta, gj = gb[:, 0:1], gb[:, 2:3], gr[0:1, :]
            gl = gb[c - 1:c, 0:1]
            inc, strict = rowi >= coli, rowi > coli
        else:
            gi, beta, gj = gb[:, 1:2], gb[:, 3:4], gr[1:2, :]
            gl = gb[0:1, 1:2]
            inc, strict = rowi <= coli, rowi < coli
        kf = k.astype(F32)
        dec = jnp.exp(jnp.where(inc, gi - gj, -1e30))
        eg = jnp.exp(gi)
        kb = kf * beta
        a = _dot_nt(jnp.concatenate([kb.astype(BF16), q], axis=0), k)
        l_mat = jnp.where(strict, a[0:c] * dec, 0.0)
        qk = jnp.where(inc, a[c:2 * c] * dec, 0.0)
        t_inv = _inv_unit_tri(l_mat)
        rhs = jnp.concatenate([v.astype(F32) * beta, kb * eg], axis=1).astype(BF16)
        uw = _dot(t_inv.astype(BF16), rhs)
        wq = jnp.concatenate([uw[:, B_HEAD_DIM:], q.astype(F32) * eg], axis=0).astype(BF16)
        ws = _dot(wq, state.astype(BF16))
        v_new = (uw[:, :B_HEAD_DIM] - ws[0:c]).astype(BF16)
        o = ws[c:2 * c] + _dot(qk.astype(BF16), v_new)
        kd = (kf * jnp.exp(gl - gi)).astype(BF16)
        state = state * jnp.exp(gl) + _dot_tn(kd, v_new)
        oacc[pl.ds(r0, c), :] += o
        return state

    def step(i, carry):
        sf, sb = carry
        return chunk(i, sf, True), chunk(n_chunks - 1 - i, sb, False)

    z0 = jnp.zeros((B_HEAD_DIM, B_HEAD_DIM), F32)
    lax.fori_loop(0, n_chunks, step, (z0, z0))

    def post(i, carry):
        r0 = pl.multiple_of(i * sl, sl)
        y = _rms(oacc[pl.ds(r0, sl), :], gout_ref[...]) * z_ref[pl.ds(r0, sl), :].astype(F32)
        o_ref[pl.ds(r0, sl), :] = y.astype(BF16)
        return carry

    lax.fori_loop(0, s // sl, post, 0)


def _gdn(main, conv_qkv, gbh, grh, gout, seq_len):
    t = main.shape[0]
    s = seq_len
    hd = B_HEAD_DIM
    qkv0 = (2 * A_WIDTH) // hd
    z0 = (2 * A_WIDTH + 3 * B_WIDTH) // hd
    kern = functools.partial(_gdn_kernel, seq_len=s)
    return pl.pallas_call(
        kern,
        out_shape=jax.ShapeDtypeStruct((t, B_WIDTH), BF16),
        grid=(t // s, B_HEADS),
        in_specs=[
            pl.BlockSpec((s, hd), lambda n, h: (n, qkv0 + h)),
            pl.BlockSpec((s, hd), lambda n, h: (n, qkv0 + B_HEADS + h)),
            pl.BlockSpec((s, hd), lambda n, h: (n, qkv0 + 2 * B_HEADS + h)),
            pl.BlockSpec((3, hd), lambda n, h: (0, h)),
            pl.BlockSpec((3, hd), lambda n, h: (0, B_HEADS + h)),
            pl.BlockSpec((3, hd), lambda n, h: (0, 2 * B_HEADS + h)),
            pl.BlockSpec((None, s, 4), lambda n, h: (h, n, 0)),
            pl.BlockSpec((None, s // B_CHUNK, 2, B_CHUNK), lambda n, h: (h, n, 0, 0)),
            pl.BlockSpec((s, hd), lambda n, h: (n, z0 + h)),
            pl.BlockSpec((1, hd), lambda n, h: (0, 0)),
        ],
        out_specs=pl.BlockSpec((s, hd), lambda n, h: (n, h)),
        scratch_shapes=[pltpu.VMEM((s, hd), BF16), pltpu.VMEM((s, hd), BF16),
                        pltpu.VMEM((s, hd), BF16), pltpu.VMEM((s, hd), F32)],
        compiler_params=pltpu.CompilerParams(
            dimension_semantics=("parallel", "parallel"), vmem_limit_bytes=VMEM_LIMIT),
        name="gdn",
    )(main, main, main, conv_qkv, conv_qkv, conv_qkv, gbh, grh, main, gout)


def _merge_kernel(u_ref, v_ref, ws_ref, bst_ref, b_ref, ga_ref, gb_ref, x_ref,
                  wa_ref, wb_ref, wo_ref, gffn_ref, x1_ref, h2_ref, a_s):
    tm = u_ref.shape[0]
    gd = A_WIDTH // A_GROUPS
    for ci in range(tm // A_CHUNK):
        rows = slice(ci * A_CHUNK, (ci + 1) * A_CHUNK)
        for g in range(A_GROUPS):
            cols = slice(g * gd, (g + 1) * gd)
            mixed = _dot(ws_ref[g], v_ref[rows, cols]) + bst_ref[:, g:g + 1]
            a_s[rows, cols] = (u_ref[rows, cols].astype(F32) * mixed).astype(BF16)
    pa = _dot(a_s[...], wa_ref[...])
    pb = _dot(b_ref[...], wb_ref[...])
    merged = ga_ref[...].astype(F32) * pa + gb_ref[...].astype(F32) * pb
    x1 = x_ref[...] + _dot(merged.astype(BF16), wo_ref[...])
    x1_ref[...] = x1
    h2_ref[...] = _rms(x1, gffn_ref[...]).astype(BF16)


def _merge(main, b, x, ws, bst, wa, wb, wo, gffn):
    t = x.shape[0]
    tm = TOKEN_TILE
    gate0 = (2 * A_WIDTH + 4 * B_WIDTH) // D_MODEL
    const = dict(pipeline_mode=pl.Buffered(1))
    return pl.pallas_call(
        _merge_kernel,
        out_shape=(jax.ShapeDtypeStruct((t, D_MODEL), F32),
                   jax.ShapeDtypeStruct((t, D_MODEL), BF16)),
        grid=(t // tm,),
        in_specs=[
            pl.BlockSpec((tm, A_WIDTH), lambda i: (i, 0)),
            pl.BlockSpec((tm, A_WIDTH), lambda i: (i, 1)),
            pl.BlockSpec((A_GROUPS, A_CHUNK, A_CHUNK), lambda i: (0, 0, 0), **const),
            pl.BlockSpec((A_CHUNK, A_GROUPS), lambda i: (0, 0), **const),
            pl.BlockSpec((tm, B_WIDTH), lambda i: (i, 0)),
            pl.BlockSpec((tm, D_MODEL), lambda i: (i, gate0)),
            pl.BlockSpec((tm, D_MODEL), lambda i: (i, gate0 + 1)),
            pl.BlockSpec((tm, D_MODEL), lambda i: (i, 0)),
            pl.BlockSpec((A_WIDTH, D_MODEL), lambda i: (0, 0), **const),
            pl.BlockSpec((B_WIDTH, D_MODEL), lambda i: (0, 0), **const),
            pl.BlockSpec((D_MODEL, D_MODEL), lambda i: (0, 0), **const),
            pl.BlockSpec((1, D_MODEL), lambda i: (0, 0), **const),
        ],
        out_specs=(
            pl.BlockSpec((tm, D_MODEL), lambda i: (i, 0)),
            pl.BlockSpec((tm, D_MODEL), lambda i: (i, 0)),
        ),
        scratch_shapes=[pltpu.VMEM((tm, A_WIDTH), BF16)],
        compiler_params=pltpu.CompilerParams(
            dimension_semantics=("parallel",), vmem_limit_bytes=VMEM_LIMIT),
        name="merge",
    )(main, main, ws, bst, b, main, main, x, wa, wb, wo, gffn)


def _ffn_kernel(h_ref, hp_ref, hn_ref, x1_ref, w1_ref, w2_ref, c1_ref, c2_ref, wd_ref,
                x2_ref, hc, acc, *, tiles_per_seq):
    i = pl.program_id(0)
    j = pl.program_id(1)
    tm = h_ref.shape[0]

    @pl.when(j == 0)
    def _():
        at_start = (i % tiles_per_seq) == 0
        at_end = (i % tiles_per_seq) == tiles_per_seq - 1
        hc[0:HALO, :] = jnp.where(at_start, jnp.zeros_like(hp_ref[...]), hp_ref[...])
        hc[HALO:HALO + tm, :] = h_ref[...]
        hc[HALO + tm:, :] = jnp.where(at_end, jnp.zeros_like(hn_ref[...]), hn_ref[...])
        acc[...] = jnp.zeros_like(acc)

    lhs = hc[...]
    n_rows = tm + 2 * HALO

    def conv(up, cw):
        prev = pltpu.roll(up, 1, 0)[HALO:HALO + tm]
        nxt = pltpu.roll(up, n_rows - 1, 0)[HALO:HALO + tm]
        return cw[0:1, :] * prev + cw[1:2, :] * up[HALO:HALO + tm] + cw[2:3, :] * nxt

    a1 = conv(_dot(lhs, w1_ref[...]), c1_ref[...])
    a2 = conv(_dot(lhs, w2_ref[...]), c2_ref[...])
    act = (a1 * jax.nn.sigmoid(a1) * a2).astype(BF16)
    acc[...] += _dot(act, wd_ref[...])

    @pl.when(j == pl.num_programs(1) - 1)
    def _():
        x2_ref[...] = x1_ref[...] + acc[...]


def _ffn(h2, x1, w_up, conv_ffn, w_down, seq_len):
    t = h2.shape[0]
    tm, tf = TOKEN_TILE, FFN_COL_TILE
    nj = D_FF // tf
    hb = tm // HALO
    last = t // HALO - 1
    kern = functools.partial(_ffn_kernel, tiles_per_seq=seq_len // tm)
    return pl.pallas_call(
        kern,
        out_shape=jax.ShapeDtypeStruct((t, D_MODEL), F32),
        grid=(t // tm, nj),
        in_specs=[
            pl.BlockSpec((tm, D_MODEL), lambda i, j: (i, 0)),
            pl.BlockSpec((HALO, D_MODEL), lambda i, j: (jnp.maximum(i * hb - 1, 0), 0)),
            pl.BlockSpec((HALO, D_MODEL), lambda i, j: (jnp.minimum((i + 1) * hb, last), 0)),
            pl.BlockSpec((tm, D_MODEL), lambda i, j: (i, 0)),
            pl.BlockSpec((D_MODEL, tf), lambda i, j: (0, j)),
            pl.BlockSpec((D_MODEL, tf), lambda i, j: (0, nj + j)),
            pl.BlockSpec((3, tf), lambda i, j: (0, j)),
            pl.BlockSpec((3, tf), lambda i, j: (0, nj + j)),
            pl.BlockSpec((tf, D_MODEL), lambda i, j: (j, 0)),
        ],
        out_specs=pl.BlockSpec((tm, D_MODEL), lambda i, j: (i, 0)),
        scratch_shapes=[pltpu.VMEM((tm + 2 * HALO, D_MODEL), BF16),
                        pltpu.VMEM((tm, D_MODEL), F32)],
        compiler_params=pltpu.CompilerParams(
            dimension_semantics=("parallel", "arbitrary"), vmem_limit_bytes=VMEM_LIMIT),
        name="ffn",
    )(h2, h2, h2, x1, w_up, w_up, conv_ffn, conv_ffn, w_down)


def _ple_final_kernel(x2_ref, p_ref, gple_ref, wg_ref, wp_ref, gfin_ref, y_ref, *, final):
    x2 = x2_ref[...]
    gate = jax.nn.sigmoid(_dot(_rms(x2, gple_ref[...]).astype(BF16), wg_ref[...]))
    x3 = x2 + gate * _dot(p_ref[...].astype(BF16), wp_ref[...])
    y_ref[...] = _rms(x3, gfin_ref[...]) if final else x3


def _ple_final(x2, p, gple, wg, wp, gfin, final):
    t = x2.shape[0]
    tm = TOKEN_TILE
    const = dict(pipeline_mode=pl.Buffered(1))
    return pl.pallas_call(
        functools.partial(_ple_final_kernel, final=final),
        out_shape=jax.ShapeDtypeStruct((t, D_MODEL), F32),
        grid=(t // tm,),
        in_specs=[
            pl.BlockSpec((tm, D_MODEL), lambda i: (i, 0)),
            pl.BlockSpec((tm, PLE_DIM), lambda i: (i, 0)),
            pl.BlockSpec((1, D_MODEL), lambda i: (0, 0), **const),
            pl.BlockSpec((D_MODEL, D_MODEL), lambda i: (0, 0), **const),
            pl.BlockSpec((PLE_DIM, D_MODEL), lambda i: (0, 0), **const),
            pl.BlockSpec((1, D_MODEL), lambda i: (0, 0), **const),
        ],
        out_specs=pl.BlockSpec((tm, D_MODEL), lambda i: (i, 0)),
        compiler_params=pltpu.CompilerParams(
            dimension_semantics=("parallel",), vmem_limit_bytes=VMEM_LIMIT),
        name="ple_final",
    )(x2, p, gple, wg, wp, gfin)


def _prep_layer(g_mix, w_in, g_a_v, w_s, b_s, conv_qkv, a_log_f, a_log_b, dt_bias_f, dt_bias_b,
                g_b_out, w_a_out, w_b_out, w_out, g_ffn, w_up, conv_ffn, w_down, g_ple,
                w_ple_gate, w_ple):
    n0 = 2 * A_WIDTH + 4 * B_WIDTH
    w_main = jnp.concatenate([w_in[:, :n0], w_in[:, n0 + N_SMALL:]], axis=1).astype(BF16)
    sm = w_in[:, n0:n0 + N_SMALL].reshape(D_MODEL, 4, B_HEADS)
    per_head = jnp.stack([sm[:, 2], sm[:, 3], sm[:, 0], sm[:, 1]], axis=-1)
    w_small = jnp.pad(per_head.reshape(D_MODEL, N_SMALL), ((0, 0), (0, 128 - N_SMALL))).astype(BF16)
    w_small_t = jnp.stack([sm[:, 2], sm[:, 3]], axis=-1).reshape(D_MODEL, 2 * B_HEADS).T.astype(BF16)
    zeros = jnp.zeros((B_HEADS,), F32)
    alog_r = jnp.stack([a_log_f, a_log_b, zeros, zeros], axis=-1).reshape(1, N_SMALL)
    dtb_r = jnp.stack([dt_bias_f, dt_bias_b, zeros, zeros], axis=-1).reshape(1, N_SMALL)
    alog_c = jnp.stack([a_log_f, a_log_b], axis=-1).reshape(2 * B_HEADS, 1)
    dtb_c = jnp.stack([dt_bias_f, dt_bias_b], axis=-1).reshape(2 * B_HEADS, 1)
    return dict(
        g_mix=g_mix.reshape(1, D_MODEL), w_main=w_main, w_small=w_small, w_small_t=w_small_t,
        g_a_v=g_a_v.reshape(1, A_WIDTH), w_s=w_s.astype(BF16), bst=b_s.T,
        conv_qkv=conv_qkv, alog_r=alog_r, dtb_r=dtb_r, alog_c=alog_c, dtb_c=dtb_c,
        g_b_out=g_b_out.reshape(1, B_HEAD_DIM), w_a_out=w_a_out.astype(BF16),
        w_b_out=w_b_out.astype(BF16), w_out=w_out.astype(BF16), g_ffn=g_ffn.reshape(1, D_MODEL),
        w_up=w_up.astype(BF16), conv_ffn=conv_ffn, w_down=w_down.astype(BF16),
        g_ple=g_ple.reshape(1, D_MODEL), w_ple_gate=w_ple_gate.astype(BF16),
        w_ple=w_ple.astype(BF16))


def _layer(x, p, w, gfin, seq_len, final):
    main, small, small_t = _in_proj(x, w["g_mix"], w["w_main"], w["w_small"], w["w_small_t"],
                                    w["g_a_v"])
    gbh, grh = _gate_prep(small, small_t, w["alog_r"], w["dtb_r"], w["alog_c"], w["dtb_c"])
    b = _gdn(main, w["conv_qkv"], gbh, grh, w["g_b_out"], seq_len)
    x1, h2 = _merge(main, b, x, w["w_s"], w["bst"], w["w_a_out"], w["w_b_out"], w["w_out"],
                    w["g_ffn"])
    x2 = _ffn(h2, x1, w["w_up"], w["conv_ffn"], w["w_down"], seq_len)
    return _ple_final(x2, p, w["g_ple"], w["w_ple_gate"], w["w_ple"], gfin, final)


def kernel(x_prompt, x_sample, p_prompt, p_sample, g_mix, w_in, g_a_v, w_s, b_s, conv_qkv, a_log_f, a_log_b, dt_bias_f, dt_bias_b, g_b_out, w_a_out, w_b_out, w_out, g_ffn, w_up, conv_ffn, w_down, g_ple, w_ple_gate, w_ple, g_final):
    depth = w_in.shape[0]
    per_layer = (g_mix, w_in, g_a_v, w_s, b_s, conv_qkv, a_log_f, a_log_b, dt_bias_f, dt_bias_b,
                 g_b_out, w_a_out, w_b_out, w_out, g_ffn, w_up, conv_ffn, w_down, g_ple,
                 w_ple_gate, w_ple)
    layers = [_prep_layer(*(a[i] for a in per_layer)) for i in range(depth)]
    gfin = g_final.reshape(1, D_MODEL)

    def trunk(x, p):
        bsz, seq, _ = x.shape
        xf = x.reshape(bsz * seq, D_MODEL)
        for i, w in enumerate(layers):
            xf = _layer(xf, p[i].reshape(bsz * seq, PLE_DIM), w, gfin, seq, i == depth - 1)
        return xf.reshape(bsz, seq, D_MODEL)

    return (trunk(x_prompt, p_prompt), trunk(x_sample, p_sample))
```

```python
import functools
import math

import jax
import jax.numpy as jnp
from jax import lax
from jax.experimental import pallas as pl
from jax.experimental.pallas import tpu as pltpu

F32 = jnp.float32
BF16 = jnp.bfloat16

D_MODEL = 2048
PLE_DIM = 256
A_WIDTH = 1024
A_GROUPS = 8
A_CHUNK = 128
B_HEADS = 8
B_HEAD_DIM = 128
B_WIDTH = B_HEADS * B_HEAD_DIM
B_CHUNK = 64
Q_SCALE = 1.0 / math.sqrt(B_HEAD_DIM)
D_FF = 4096
EPS = 1e-6

N_MAIN = 4 * B_WIDTH + 2 * D_MODEL
N_SMALL = 4 * B_HEADS

TOKEN_TILE = 512
PROJ_COL_TILE = 1024
N_QKV_TILES = 3 * B_WIDTH // PROJ_COL_TILE
FFN_COL_TILE = 1024
HALO = 16
GDN_SLAB = 256
GDN_GROUP = 4
VMEM_LIMIT = 60000 * 1024


def _rms(x, g):
    return x * lax.rsqrt(jnp.mean(x * x, axis=-1, keepdims=True) + EPS) * g


def _gelu(x):
    return 0.5 * x * (1.0 + lax.erf(x * math.sqrt(0.5)))


def _dot(a, b):
    return jnp.dot(a, b, preferred_element_type=F32)


def _dot_nt(a, b):
    return lax.dot_general(a, b, (((1,), (1,)), ((), ())), preferred_element_type=F32)


def _dot_tn(a, b):
    return lax.dot_general(a, b, (((0,), (0,)), ((), ())), preferred_element_type=F32)


def _split3(x):
    hi = x.astype(BF16)
    r1 = x - hi.astype(F32)
    mid = r1.astype(BF16)
    lo = (r1 - mid.astype(F32)).astype(BF16)
    return hi, mid, lo


def _in_proj_a_kernel(x_ref, gmix_ref, w_ref, wsm_ref, wsmt_ref, gav_ref,
                      h_ref, uv_ref, small_ref, smallt_ref):
    h = _rms(x_ref[...], gmix_ref[...]).astype(BF16)
    h_ref[...] = h
    small_ref[...] = _dot(h, wsm_ref[...])
    smallt_ref[...] = _dot_nt(wsmt_ref[...], h)
    uv_ref[:, :A_WIDTH] = _gelu(_dot(h, w_ref[:, :A_WIDTH])).astype(BF16)
    v = _gelu(_dot(h, w_ref[:, A_WIDTH:]))
    uv_ref[:, A_WIDTH:] = _rms(v, gav_ref[...]).astype(BF16)


def _in_proj_a(x, gmix, w_uv, w_small, w_small_t, gav):
    t = x.shape[0]
    tm = TOKEN_TILE
    const = dict(pipeline_mode=pl.Buffered(1))
    return pl.pallas_call(
        _in_proj_a_kernel,
        out_shape=(jax.ShapeDtypeStruct((t, D_MODEL), BF16),
                   jax.ShapeDtypeStruct((t, 2 * A_WIDTH), BF16),
                   jax.ShapeDtypeStruct((t, 128), F32),
                   jax.ShapeDtypeStruct((2 * B_HEADS, t), F32)),
        grid=(t // tm,),
        in_specs=[
            pl.BlockSpec((tm, D_MODEL), lambda i: (i, 0)),
            pl.BlockSpec((1, D_MODEL), lambda i: (0, 0), **const),
            pl.BlockSpec((D_MODEL, 2 * A_WIDTH), lambda i: (0, 0), **const),
            pl.BlockSpec((D_MODEL, 128), lambda i: (0, 0), **const),
            pl.BlockSpec((2 * B_HEADS, D_MODEL), lambda i: (0, 0), **const),
            pl.BlockSpec((1, A_WIDTH), lambda i: (0, 0), **const),
        ],
        out_specs=(
            pl.BlockSpec((tm, D_MODEL), lambda i: (i, 0)),
            pl.BlockSpec((tm, 2 * A_WIDTH), lambda i: (i, 0)),
            pl.BlockSpec((tm, 128), lambda i: (i, 0)),
            pl.BlockSpec((2 * B_HEADS, tm), lambda i: (0, i)),
        ),
        compiler_params=pltpu.CompilerParams(
            dimension_semantics=("parallel",), vmem_limit_bytes=VMEM_LIMIT),
        name="in_proj_a",
    )(x, gmix, w_uv, w_small, w_small_t, gav)


def _in_proj_b_kernel(h_ref, w_ref, out_ref):
    j = pl.program_id(1)
    acc = _dot(h_ref[...], w_ref[...])
    sig = jax.nn.sigmoid(acc)
    gated = sig * jnp.where(j == N_QKV_TILES, acc, 1.0)
    out_ref[...] = jnp.where(j < N_QKV_TILES, acc, gated).astype(BF16)


def _in_proj_b(h, w_main):
    t = h.shape[0]
    tm, tn = TOKEN_TILE, PROJ_COL_TILE
    return pl.pallas_call(
        _in_proj_b_kernel,
        out_shape=jax.ShapeDtypeStruct((t, N_MAIN), BF16),
        grid=(t // tm, N_MAIN // tn),
        in_specs=[
            pl.BlockSpec((tm, D_MODEL), lambda i, j: (i, 0)),
            pl.BlockSpec((D_MODEL, tn), lambda i, j: (0, j)),
        ],
        out_specs=pl.BlockSpec((tm, tn), lambda i, j: (i, j)),
        compiler_params=pltpu.CompilerParams(
            dimension_semantics=("parallel", "arbitrary"), vmem_limit_bytes=VMEM_LIMIT),
        name="in_proj_b",
    )(h, w_main)


def _softplus(x):
    return jnp.maximum(x, 0.0) + jnp.log1p(jnp.exp(-jnp.abs(x)))


def _gate_prep_kernel(small_ref, smallt_ref, alog_r, dtb_r, alog_c, dtb_c, gbh_ref, grh_ref):
    tb = small_ref.shape[0]
    nl = N_SMALL
    row = lax.broadcasted_iota(jnp.int32, (tb, tb), 0)
    col = lax.broadcasted_iota(jnp.int32, (tb, tb), 1)
    same = (row // B_CHUNK) == (col // B_CHUNK)
    m_lo = jnp.where(same & (col <= row), 1.0, 0.0).astype(BF16)
    m_up = jnp.where(same & (col >= row), 1.0, 0.0).astype(BF16)

    sm = small_ref[:, 0:nl]
    g = -jnp.exp(alog_r[...]) * _softplus(sm + dtb_r[...])
    g3 = jnp.concatenate(_split3(g), axis=1)
    pf = _dot(m_lo, g3)
    pb = _dot(m_up, g3)
    pf = pf[:, 0:nl] + pf[:, nl:2 * nl] + pf[:, 2 * nl:3 * nl]
    pb = pb[:, 0:nl] + pb[:, nl:2 * nl] + pb[:, 2 * nl:3 * nl]
    lane = lax.broadcasted_iota(jnp.int32, (tb, nl), 1) % 4
    vals = jnp.where(lane == 0, pf, jnp.where(lane == 1, pb, jax.nn.sigmoid(sm)))
    for h in range(B_HEADS):
        gbh_ref[h] = vals[:, 4 * h:4 * h + 4]

    gt = -jnp.exp(alog_c[...]) * _softplus(smallt_ref[...] + dtb_c[...])
    gt3 = jnp.concatenate(_split3(gt), axis=0)
    nr = 2 * B_HEADS
    rf = _dot(gt3, m_up)
    rb = _dot(gt3, m_lo)
    rf = rf[0:nr] + rf[nr:2 * nr] + rf[2 * nr:3 * nr]
    rb = rb[0:nr] + rb[nr:2 * nr] + rb[2 * nr:3 * nr]
    sub = lax.broadcasted_iota(jnp.int32, (nr, tb), 0) % 2
    grow = jnp.where(sub == 0, rf, rb)
    for c in range(tb // B_CHUNK):
        for h in range(B_HEADS):
            grh_ref[h, c, :, 0:B_CHUNK] = grow[2 * h:2 * h + 2, c * B_CHUNK:(c + 1) * B_CHUNK]
            grh_ref[h, c, :, B_CHUNK:] = jnp.zeros((2, 128 - B_CHUNK), F32)


def _gate_prep(small, small_t, alog_r, dtb_r, alog_c, dtb_c):
    t = small.shape[0]
    tb = TOKEN_TILE
    nc = tb // B_CHUNK
    return pl.pallas_call(
        _gate_prep_kernel,
        out_shape=(jax.ShapeDtypeStruct((B_HEADS, t, 4), F32),
                   jax.ShapeDtypeStruct((B_HEADS, t // B_CHUNK, 2, 128), F32)),
        grid=(t // tb,),
        in_specs=[
            pl.BlockSpec((tb, 128), lambda i: (i, 0)),
            pl.BlockSpec((2 * B_HEADS, tb), lambda i: (0, i)),
            pl.BlockSpec((1, N_SMALL), lambda i: (0, 0)),
            pl.BlockSpec((1, N_SMALL), lambda i: (0, 0)),
            pl.BlockSpec((2 * B_HEADS, 1), lambda i: (0, 0)),
            pl.BlockSpec((2 * B_HEADS, 1), lambda i: (0, 0)),
        ],
        out_specs=(
            pl.BlockSpec((B_HEADS, tb, 4), lambda i: (0, i, 0)),
            pl.BlockSpec((B_HEADS, nc, 2, 128), lambda i: (0, i, 0, 0)),
        ),
        compiler_params=pltpu.CompilerParams(
            dimension_semantics=("parallel",), vmem_limit_bytes=VMEM_LIMIT),
        name="gate_prep",
    )(small, small_t, alog_r, dtb_r, alog_c, dtb_c)


def _gdn_kernel(q_ref, k_ref, v_ref, cq_ref, ck_ref, cv_ref, gbh_ref, grh_ref, z_ref, gout_ref,
                o_ref, qs, ks, vs, oacc, lhs_s, b_s, ob_s, egl_s, *, seq_len):
    s = seq_len
    c = B_CHUNK
    n_chunks = s // c
    sl = min(GDN_SLAB, s)
    rows_sl = lax.broadcasted_iota(jnp.int32, (sl, B_HEAD_DIM), 0)

    def conv_silu(x_ref, c_ref, r0):
        xm = x_ref[pl.ds(r0, sl), :].astype(F32)
        p0 = pl.multiple_of(jnp.maximum(r0 - HALO, 0), HALO)
        n0 = pl.multiple_of(jnp.minimum(r0 + sl, s - HALO), HALO)
        prev = x_ref[pl.ds(p0, HALO), :].astype(F32)[HALO - 1:HALO, :]
        nxt = x_ref[pl.ds(n0, HALO), :].astype(F32)[0:1, :]
        prev = jnp.where(r0 > 0, prev, 0.0)
        nxt = jnp.where(r0 + sl < s, nxt, 0.0)
        xp = jnp.where(rows_sl == 0, prev, pltpu.roll(xm, 1, 0))
        xn = jnp.where(rows_sl == sl - 1, nxt, pltpu.roll(xm, sl - 1, 0))
        w = c_ref[...]
        y = w[0:1, :] * xp + w[1:2, :] * xm + w[2:3, :] * xn
        return y * jax.nn.sigmoid(y)

    def l2n(y):
        return y * lax.rsqrt(jnp.sum(y * y, axis=-1, keepdims=True) + EPS)

    def pre(i, carry):
        r0 = pl.multiple_of(i * sl, sl)
        qs[pl.ds(r0, sl), :] = (l2n(conv_silu(q_ref, cq_ref, r0)) * Q_SCALE).astype(BF16)
        ks[pl.ds(r0, sl), :] = l2n(conv_silu(k_ref, ck_ref, r0)).astype(BF16)
        vs[pl.ds(r0, sl), :] = conv_silu(v_ref, cv_ref, r0).astype(BF16)
        oacc[pl.ds(r0, sl), :] = jnp.zeros((sl, B_HEAD_DIM), F32)
        return carry

    lax.fori_loop(0, s // sl, pre, 0)

    rowi = lax.broadcasted_iota(jnp.int32, (c, 2 * c), 0)
    coli = lax.broadcasted_iota(jnp.int32, (c, 2 * c), 1)
    zk = jnp.zeros((c, B_HEAD_DIM), BF16)
    zw = jnp.zeros((c, 2 * B_HEAD_DIM), BF16)
    zw2 = jnp.zeros((c, 4 * c), BF16)

    grp = GDN_GROUP
    n_groups = n_chunks // grp
    per_slot = 2 * grp

    def group_chunks(gi):
        out = []
        for g in range(grp):
            out += [(gi * grp + g, True), (n_chunks - 1 - (gi * grp + g), False)]
        return out

    def prep_stages(gi, slot):
        st = []
        for ci, fwd in group_chunks(gi):
            r0 = pl.multiple_of(ci * c, c)
            q = qs[pl.ds(r0, c), :]
            k = ks[pl.ds(r0, c), :]
            v = vs[pl.ds(r0, c), :]
            gb = gbh_ref[pl.ds(r0, c), :]
            gr = grh_ref[ci]
            if fwd:
                gi_, beta, gj = gb[:, 0:1], gb[:, 2:3], gr[0:1, :]
                gl = gb[c - 1:c, 0:1]
                inc, strict = rowi >= coli, rowi > coli
            else:
                gi_, beta, gj = gb[:, 1:2], gb[:, 3:4], gr[1:2, :]
                gl = gb[0:1, 1:2]
                inc, strict = (rowi <= coli) & (coli < c), (rowi < coli) & (coli < c)
            kf = k.astype(F32)
            kb = kf * beta
            st.append(dict(q=q, k=k, v=v, gi=gi_, beta=beta, gl=gl, inc=inc, strict=strict,
                           kf=kf, kb=kb, eg=jnp.exp(gi_),
                           dec=jnp.exp(jnp.where(inc, gi_ - gj, -1e30))))
        for d in st:
            d["a"] = _dot_nt(jnp.concatenate([d["kb"].astype(BF16), d["q"]], axis=0),
                             jnp.concatenate([d["k"], zk], axis=0))
        yield
        for d in st:
            l_pad = jnp.where(d["strict"], d["a"][0:c] * d["dec"], 0.0)
            d["qk"] = jnp.where(d["inc"], d["a"][c:2 * c] * d["dec"], 0.0).astype(BF16)
            d["px"] = jnp.where(coli == rowi + c, 1.0, 0.0) - l_pad
        for _ in range(int(math.log2(c))):
            for d in st:
                hi = d["px"].astype(BF16)
                lo = (d["px"] - hi.astype(F32)).astype(BF16)
                rhs = jnp.concatenate([jnp.concatenate([hi, lo], axis=1), zw2], axis=0)
                d["o"] = _dot(jnp.concatenate([hi, lo], axis=0), rhs)
            yield
            for d in st:
                o = d["o"]
                d["px"] = (o[:c, :2 * c] + o[:c, 2 * c:] + o[c:, :2 * c] + o[c:, 2 * c:]
                           + jnp.where(coli >= c, d["px"], 0.0))
        for d in st:
            rhs = jnp.concatenate([d["kb"] * d["eg"], d["v"].astype(F32) * d["beta"]], axis=1)
            d["wu"] = _dot(d["px"].astype(BF16),
                           jnp.concatenate([zw, rhs.astype(BF16)], axis=0)).astype(BF16)
        yield
        for d in st:
            kd = (d["kf"] * jnp.exp(d["gl"] - d["gi"])).astype(BF16)
            d["m1"] = _dot_tn(kd, d["wu"])
        yield
        for d in st:
            d["m2"] = _dot(d["qk"], jnp.concatenate([d["wu"], zw], axis=0))
        yield
        for j, d in enumerate(st):
            idx = slot * per_slot + j
            qe = d["q"].astype(F32) * d["eg"] - d["m2"][:, :B_HEAD_DIM]
            lhs_s[idx] = jnp.concatenate([-d["m1"][:, :B_HEAD_DIM], qe], axis=0).astype(BF16)
            b_s[idx] = d["m1"][:, B_HEAD_DIM:]
            ob_s[idx] = d["m2"][:, B_HEAD_DIM:]
            egl_s[idx] = jnp.broadcast_to(jnp.exp(d["gl"]), (8, B_HEAD_DIM))

    def scan_step(gi, slot, g, states):
        new = []
        for (ci, _), state, j in zip(group_chunks(gi)[2 * g:2 * g + 2], states, (2 * g, 2 * g + 1)):
            idx = slot * per_slot + j
            r0 = pl.multiple_of(ci * c, c)
            res = _dot(lhs_s[idx], state.astype(BF16))
            oacc[pl.ds(r0, c), :] += res[B_HEAD_DIM:] + ob_s[idx]
            new.append(egl_s[idx][0:1, :] * state + res[:B_HEAD_DIM] + b_s[idx])
        return tuple(new)

    for _ in prep_stages(0, 0):
        pass

    def step(i, states):
        slot = i % 2
        g = 0
        for k_stage, _ in enumerate(prep_stages(i + 1, 1 - slot)):
            if k_stage % 2 == 0 and g < grp:
                states = scan_step(i, slot, g, states)
                g += 1
        assert g == grp
        return states

    z0 = jnp.zeros((B_HEAD_DIM, B_HEAD_DIM), F32)
    states = lax.fori_loop(0, n_groups - 1, step, (z0, z0))
    for g in range(grp):
        states = scan_step(n_groups - 1, (n_groups - 1) % 2, g, states)

    def post(i, carry):
        r0 = pl.multiple_of(i * sl, sl)
        y = _rms(oacc[pl.ds(r0, sl), :], gout_ref[...]) * z_ref[pl.ds(r0, sl), :].astype(F32)
        o_ref[pl.ds(r0, sl), :] = y.astype(BF16)
        return carry

    lax.fori_loop(0, s // sl, post, 0)


def _gdn(main, conv_qkv, gbh, grh, gout, seq_len):
    t = main.shape[0]
    s = seq_len
    hd = B_HEAD_DIM
    qkv0 = 0
    z0 = 3 * B_WIDTH // hd
    kern = functools.partial(_gdn_kernel, seq_len=s)
    n_slots = 2 * 2 * GDN_GROUP
    return pl.pallas_call(
        kern,
        out_shape=jax.ShapeDtypeStruct((t, B_WIDTH), BF16),
        grid=(t // s, B_HEADS),
        in_specs=[
            pl.BlockSpec((s, hd), lambda n, h: (n, qkv0 + h)),
            pl.BlockSpec((s, hd), lambda n, h: (n, qkv0 + B_HEADS + h)),
            pl.BlockSpec((s, hd), lambda n, h: (n, qkv0 + 2 * B_HEADS + h)),
            pl.BlockSpec((3, hd), lambda n, h: (0, h)),
            pl.BlockSpec((3, hd), lambda n, h: (0, B_HEADS + h)),
            pl.BlockSpec((3, hd), lambda n, h: (0, 2 * B_HEADS + h)),
            pl.BlockSpec((None, s, 4), lambda n, h: (h, n, 0)),
            pl.BlockSpec((None, s // B_CHUNK, 2, 128), lambda n, h: (h, n, 0, 0)),
            pl.BlockSpec((s, hd), lambda n, h: (n, z0 + h)),
            pl.BlockSpec((1, hd), lambda n, h: (0, 0)),
        ],
        out_specs=pl.BlockSpec((s, hd), lambda n, h: (n, h)),
        scratch_shapes=[pltpu.VMEM((s, hd), BF16), pltpu.VMEM((s, hd), BF16),
                        pltpu.VMEM((s, hd), BF16), pltpu.VMEM((s, hd), F32),
                        pltpu.VMEM((n_slots, hd + B_CHUNK, hd), BF16),
                        pltpu.VMEM((n_slots, hd, hd), F32),
                        pltpu.VMEM((n_slots, B_CHUNK, hd), F32),
                        pltpu.VMEM((n_slots, 8, hd), F32)],
        compiler_params=pltpu.CompilerParams(
            dimension_semantics=("parallel", "parallel"), vmem_limit_bytes=VMEM_LIMIT),
        name="gdn",
    )(main, main, main, conv_qkv, conv_qkv, conv_qkv, gbh, grh, main, gout)


def _merge_kernel(u_ref, v_ref, ws_ref, bst_ref, b_ref, ga_ref, gb_ref, x_ref,
                  wa_ref, wb_ref, wo_ref, gffn_ref, x1_ref, h2_ref, a_s):
    tm = u_ref.shape[0]
    gd = A_WIDTH // A_GROUPS
    for ci in range(tm // A_CHUNK):
        rows = slice(ci * A_CHUNK, (ci + 1) * A_CHUNK)
        for g in range(A_GROUPS):
            cols = slice(g * gd, (g + 1) * gd)
            mixed = _dot(ws_ref[g], v_ref[rows, cols]) + bst_ref[:, g:g + 1]
            a_s[rows, cols] = (u_ref[rows, cols].astype(F32) * mixed).astype(BF16)
    pa = _dot(a_s[...], wa_ref[...])
    pb = _dot(b_ref[...], wb_ref[...])
    merged = ga_ref[...].astype(F32) * pa + gb_ref[...].astype(F32) * pb
    x1 = x_ref[...] + _dot(merged.astype(BF16), wo_ref[...])
    x1_ref[...] = x1
    h2_ref[...] = _rms(x1, gffn_ref[...]).astype(BF16)


def _merge(uv, main, b, x, ws, bst, wa, wb, wo, gffn):
    t = x.shape[0]
    tm = TOKEN_TILE
    gate0 = 4 * B_WIDTH // D_MODEL
    const = dict(pipeline_mode=pl.Buffered(1))
    return pl.pallas_call(
        _merge_kernel,
        out_shape=(jax.ShapeDtypeStruct((t, D_MODEL), F32),
                   jax.ShapeDtypeStruct((t, D_MODEL), BF16)),
        grid=(t // tm,),
        in_specs=[
            pl.BlockSpec((tm, A_WIDTH), lambda i: (i, 0)),
            pl.BlockSpec((tm, A_WIDTH), lambda i: (i, 1)),
            pl.BlockSpec((A_GROUPS, A_CHUNK, A_CHUNK), lambda i: (0, 0, 0), **const),
            pl.BlockSpec((A_CHUNK, A_GROUPS), lambda i: (0, 0), **const),
            pl.BlockSpec((tm, B_WIDTH), lambda i: (i, 0)),
            pl.BlockSpec((tm, D_MODEL), lambda i: (i, gate0)),
            pl.BlockSpec((tm, D_MODEL), lambda i: (i, gate0 + 1)),
            pl.BlockSpec((tm, D_MODEL), lambda i: (i, 0)),
            pl.BlockSpec((A_WIDTH, D_MODEL), lambda i: (0, 0), **const),
            pl.BlockSpec((B_WIDTH, D_MODEL), lambda i: (0, 0), **const),
            pl.BlockSpec((D_MODEL, D_MODEL), lambda i: (0, 0), **const),
            pl.BlockSpec((1, D_MODEL), lambda i: (0, 0), **const),
        ],
        out_specs=(
            pl.BlockSpec((tm, D_MODEL), lambda i: (i, 0)),
            pl.BlockSpec((tm, D_MODEL), lambda i: (i, 0)),
        ),
        scratch_shapes=[pltpu.VMEM((tm, A_WIDTH), BF16)],
        compiler_params=pltpu.CompilerParams(
            dimension_semantics=("parallel",), vmem_limit_bytes=VMEM_LIMIT),
        name="merge",
    )(uv, uv, ws, bst, b, main, main, x, wa, wb, wo, gffn)


def _ffn_kernel(h_ref, hp_ref, hn_ref, x1_ref, w1_ref, w2_ref, c1_ref, c2_ref, wd_ref,
                x2_ref, hc, acc, *, tiles_per_seq):
    i = pl.program_id(0)
    j = pl.program_id(1)
    tm = h_ref.shape[0]

    @pl.when(j == 0)
    def _():
        at_start = (i % tiles_per_seq) == 0
        at_end = (i % tiles_per_seq) == tiles_per_seq - 1
        hc[0:HALO, :] = jnp.where(at_start, jnp.zeros_like(hp_ref[...]), hp_ref[...])
        hc[HALO:HALO + tm, :] = h_ref[...]
        hc[HALO + tm:, :] = jnp.where(at_end, jnp.zeros_like(hn_ref[...]), hn_ref[...])
        acc[...] = jnp.zeros_like(acc)

    lhs = hc[...]
    n_rows = tm + 2 * HALO

    def conv(up, cw):
        prev = pltpu.roll(up, 1, 0)[HALO:HALO + tm]
        nxt = pltpu.roll(up, n_rows - 1, 0)[HALO:HALO + tm]
        return cw[0:1, :] * prev + cw[1:2, :] * up[HALO:HALO + tm] + cw[2:3, :] * nxt

    a1 = conv(_dot(lhs, w1_ref[...]), c1_ref[...])
    a2 = conv(_dot(lhs, w2_ref[...]), c2_ref[...])
    act = (a1 * jax.nn.sigmoid(a1) * a2).astype(BF16)
    acc[...] += _dot(act, wd_ref[...])

    @pl.when(j == pl.num_programs(1) - 1)
    def _():
        x2_ref[...] = x1_ref[...] + acc[...]


def _ffn(h2, x1, w_up, conv_ffn, w_down, seq_len):
    t = h2.shape[0]
    tm, tf = TOKEN_TILE, FFN_COL_TILE
    nj = D_FF // tf
    hb = tm // HALO
    last = t // HALO - 1
    kern = functools.partial(_ffn_kernel, tiles_per_seq=seq_len // tm)
    return pl.pallas_call(
        kern,
        out_shape=jax.ShapeDtypeStruct((t, D_MODEL), F32),
        grid=(t // tm, nj),
        in_specs=[
            pl.BlockSpec((tm, D_MODEL), lambda i, j: (i, 0)),
            pl.BlockSpec((HALO, D_MODEL), lambda i, j: (jnp.maximum(i * hb - 1, 0), 0)),
            pl.BlockSpec((HALO, D_MODEL), lambda i, j: (jnp.minimum((i + 1) * hb, last), 0)),
            pl.BlockSpec((tm, D_MODEL), lambda i, j: (i, 0)),
            pl.BlockSpec((D_MODEL, tf), lambda i, j: (0, j)),
            pl.BlockSpec((D_MODEL, tf), lambda i, j: (0, nj + j)),
            pl.BlockSpec((3, tf), lambda i, j: (0, j)),
            pl.BlockSpec((3, tf), lambda i, j: (0, nj + j)),
            pl.BlockSpec((tf, D_MODEL), lambda i, j: (j, 0)),
        ],
        out_specs=pl.BlockSpec((tm, D_MODEL), lambda i, j: (i, 0)),
        scratch_shapes=[pltpu.VMEM((tm + 2 * HALO, D_MODEL), BF16),
                        pltpu.VMEM((tm, D_MODEL), F32)],
        compiler_params=pltpu.CompilerParams(
            dimension_semantics=("parallel", "arbitrary"), vmem_limit_bytes=VMEM_LIMIT),
        name="ffn",
    )(h2, h2, h2, x1, w_up, w_up, conv_ffn, conv_ffn, w_down)


def _ple_final_kernel(x2_ref, p_ref, gple_ref, wg_ref, wp_ref, gfin_ref, y_ref, *, final):
    x2 = x2_ref[...]
    gate = jax.nn.sigmoid(_dot(_rms(x2, gple_ref[...]).astype(BF16), wg_ref[...]))
    x3 = x2 + gate * _dot(p_ref[...].astype(BF16), wp_ref[...])
    y_ref[...] = _rms(x3, gfin_ref[...]) if final else x3


def _ple_final(x2, p, gple, wg, wp, gfin, final):
    t = x2.shape[0]
    tm = TOKEN_TILE
    const = dict(pipeline_mode=pl.Buffered(1))
    return pl.pallas_call(
        functools.partial(_ple_final_kernel, final=final),
        out_shape=jax.ShapeDtypeStruct((t, D_MODEL), F32),
        grid=(t // tm,),
        in_specs=[
            pl.BlockSpec((tm, D_MODEL), lambda i: (i, 0)),
            pl.BlockSpec((tm, PLE_DIM), lambda i: (i, 0)),
            pl.BlockSpec((1, D_MODEL), lambda i: (0, 0), **const),
            pl.BlockSpec((D_MODEL, D_MODEL), lambda i: (0, 0), **const),
            pl.BlockSpec((PLE_DIM, D_MODEL), lambda i: (0, 0), **const),
            pl.BlockSpec((1, D_MODEL), lambda i: (0, 0), **const),
        ],
        out_specs=pl.BlockSpec((tm, D_MODEL), lambda i: (i, 0)),
        compiler_params=pltpu.CompilerParams(
            dimension_semantics=("parallel",), vmem_limit_bytes=VMEM_LIMIT),
        name="ple_final",
    )(x2, p, gple, wg, wp, gfin)


def _prep_layer(g_mix, w_in, g_a_v, w_s, b_s, conv_qkv, a_log_f, a_log_b, dt_bias_f, dt_bias_b,
                g_b_out, w_a_out, w_b_out, w_out, g_ffn, w_up, conv_ffn, w_down, g_ple,
                w_ple_gate, w_ple):
    n0 = 2 * A_WIDTH + 4 * B_WIDTH
    w_uv = w_in[:, :2 * A_WIDTH].astype(BF16)
    w_main = jnp.concatenate([w_in[:, 2 * A_WIDTH:n0], w_in[:, n0 + N_SMALL:]], axis=1).astype(BF16)
    sm = w_in[:, n0:n0 + N_SMALL].reshape(D_MODEL, 4, B_HEADS)
    per_head = jnp.stack([sm[:, 2], sm[:, 3], sm[:, 0], sm[:, 1]], axis=-1)
    w_small = jnp.pad(per_head.reshape(D_MODEL, N_SMALL), ((0, 0), (0, 128 - N_SMALL))).astype(BF16)
    w_small_t = jnp.stack([sm[:, 2], sm[:, 3]], axis=-1).reshape(D_MODEL, 2 * B_HEADS).T.astype(BF16)
    zeros = jnp.zeros((B_HEADS,), F32)
    alog_r = jnp.stack([a_log_f, a_log_b, zeros, zeros], axis=-1).reshape(1, N_SMALL)
    dtb_r = jnp.stack([dt_bias_f, dt_bias_b, zeros, zeros], axis=-1).reshape(1, N_SMALL)
    alog_c = jnp.stack([a_log_f, a_log_b], axis=-1).reshape(2 * B_HEADS, 1)
    dtb_c = jnp.stack([dt_bias_f, dt_bias_b], axis=-1).reshape(2 * B_HEADS, 1)
    return dict(
        g_mix=g_mix.reshape(1, D_MODEL), w_uv=w_uv, w_main=w_main, w_small=w_small,
        w_small_t=w_small_t,
        g_a_v=g_a_v.reshape(1, A_WIDTH), w_s=w_s.astype(BF16), bst=b_s.T,
        conv_qkv=conv_qkv, alog_r=alog_r, dtb_r=dtb_r, alog_c=alog_c, dtb_c=dtb_c,
        g_b_out=g_b_out.reshape(1, B_HEAD_DIM), w_a_out=w_a_out.astype(BF16),
        w_b_out=w_b_out.astype(BF16), w_out=w_out.astype(BF16), g_ffn=g_ffn.reshape(1, D_MODEL),
        w_up=w_up.astype(BF16), conv_ffn=conv_ffn, w_down=w_down.astype(BF16),
        g_ple=g_ple.reshape(1, D_MODEL), w_ple_gate=w_ple_gate.astype(BF16),
        w_ple=w_ple.astype(BF16))


def _layer(x, p, w, gfin, seq_len, final):
    h, uv, small, small_t = _in_proj_a(x, w["g_mix"], w["w_uv"], w["w_small"], w["w_small_t"],
                                       w["g_a_v"])
    main = _in_proj_b(h, w["w_main"])
    gbh, grh = _gate_prep(small, small_t, w["alog_r"], w["dtb_r"], w["alog_c"], w["dtb_c"])
    b = _gdn(main, w["conv_qkv"], gbh, grh, w["g_b_out"], seq_len)
    x1, h2 = _merge(uv, main, b, x, w["w_s"], w["bst"], w["w_a_out"], w["w_b_out"], w["w_out"],
                    w["g_ffn"])
    x2 = _ffn(h2, x1, w["w_up"], w["conv_ffn"], w["w_down"], seq_len)
    return _ple_final(x2, p, w["g_ple"], w["w_ple_gate"], w["w_ple"], gfin, final)


def kernel(x_prompt, x_sample, p_prompt, p_sample, g_mix, w_in, g_a_v, w_s, b_s, conv_qkv, a_log_f, a_log_b, dt_bias_f, dt_bias_b, g_b_out, w_a_out, w_b_out, w_out, g_ffn, w_up, conv_ffn, w_down, g_ple, w_ple_gate, w_ple, g_final):
    depth = w_in.shape[0]
    per_layer = (g_mix, w_in, g_a_v, w_s, b_s, conv_qkv, a_log_f, a_log_b, dt_bias_f, dt_bias_b,
                 g_b_out, w_a_out, w_b_out, w_out, g_ffn, w_up, conv_ffn, w_down, g_ple,
                 w_ple_gate, w_ple)
    layers = [_prep_layer(*(a[i] for a in per_layer)) for i in range(depth)]
    gfin = g_final.reshape(1, D_MODEL)

    def trunk(x, p):
        bsz, seq, _ = x.shape
        xf = x.reshape(bsz * seq, D_MODEL)
        for i, w in enumerate(layers):
            xf = _layer(xf, p[i].reshape(bsz * seq, PLE_DIM), w, gfin, seq, i == depth - 1)
        return xf.reshape(bsz, seq, D_MODEL)

    return (trunk(x_prompt, p_prompt), trunk(x_sample, p_sample))
```

```python
import functools
import math

import jax
import jax.numpy as jnp
from jax import lax
from jax.experimental import pallas as pl
from jax.experimental.pallas import tpu as pltpu

F32 = jnp.float32
BF16 = jnp.bfloat16

D_MODEL = 2048
PLE_DIM = 256
A_WIDTH = 1024
A_GROUPS = 8
A_CHUNK = 128
B_HEADS = 8
B_HEAD_DIM = 128
B_WIDTH = B_HEADS * B_HEAD_DIM
B_CHUNK = 64
Q_SCALE = 1.0 / math.sqrt(B_HEAD_DIM)
D_FF = 4096
EPS = 1e-6

N_MAIN = 2 * D_MODEL + B_WIDTH
N_SMALL = 4 * B_HEADS

TOKEN_TILE = 512
PROJ_ROW_TILE = 1024
PROJ_COL_TILE = 1024
N_GATE_TILES = 2 * D_MODEL // PROJ_COL_TILE
FFN_COL_TILE = 1024
HALO = 16
GDN_SLAB = 256
GDN_GROUP = 4
VMEM_LIMIT = 60000 * 1024


def _rms(x, g):
    return x * lax.rsqrt(jnp.mean(x * x, axis=-1, keepdims=True) + EPS) * g


def _gelu(x):
    return 0.5 * x * (1.0 + lax.erf(x * math.sqrt(0.5)))


def _dot(a, b):
    return jnp.dot(a, b, preferred_element_type=F32)


def _dot_nt(a, b):
    return lax.dot_general(a, b, (((1,), (1,)), ((), ())), preferred_element_type=F32)


def _dot_tn(a, b):
    return lax.dot_general(a, b, (((0,), (0,)), ((), ())), preferred_element_type=F32)


def _split3(x):
    hi = x.astype(BF16)
    r1 = x - hi.astype(F32)
    mid = r1.astype(BF16)
    lo = (r1 - mid.astype(F32)).astype(BF16)
    return hi, mid, lo


def _in_proj_a_kernel(x_ref, gmix_ref, w_ref, wsm_ref, wsmt_ref, gav_ref,
                      h_ref, uv_ref, small_ref, smallt_ref):
    h = _rms(x_ref[...], gmix_ref[...]).astype(BF16)
    h_ref[...] = h
    small_ref[...] = _dot(h, wsm_ref[...])
    smallt_ref[...] = _dot_nt(wsmt_ref[...], h)
    uv_ref[:, :A_WIDTH] = _gelu(_dot(h, w_ref[:, :A_WIDTH])).astype(BF16)
    v = _gelu(_dot(h, w_ref[:, A_WIDTH:]))
    uv_ref[:, A_WIDTH:] = _rms(v, gav_ref[...]).astype(BF16)


def _in_proj_a(x, gmix, w_uv, w_small, w_small_t, gav):
    t = x.shape[0]
    tm = TOKEN_TILE
    const = dict(pipeline_mode=pl.Buffered(1))
    return pl.pallas_call(
        _in_proj_a_kernel,
        out_shape=(jax.ShapeDtypeStruct((t, D_MODEL), BF16),
                   jax.ShapeDtypeStruct((t, 2 * A_WIDTH), BF16),
                   jax.ShapeDtypeStruct((t, 128), F32),
                   jax.ShapeDtypeStruct((2 * B_HEADS, t), F32)),
        grid=(t // tm,),
        in_specs=[
            pl.BlockSpec((tm, D_MODEL), lambda i: (i, 0)),
            pl.BlockSpec((1, D_MODEL), lambda i: (0, 0), **const),
            pl.BlockSpec((D_MODEL, 2 * A_WIDTH), lambda i: (0, 0), **const),
            pl.BlockSpec((D_MODEL, 128), lambda i: (0, 0), **const),
            pl.BlockSpec((2 * B_HEADS, D_MODEL), lambda i: (0, 0), **const),
            pl.BlockSpec((1, A_WIDTH), lambda i: (0, 0), **const),
        ],
        out_specs=(
            pl.BlockSpec((tm, D_MODEL), lambda i: (i, 0)),
            pl.BlockSpec((tm, 2 * A_WIDTH), lambda i: (i, 0)),
            pl.BlockSpec((tm, 128), lambda i: (i, 0)),
            pl.BlockSpec((2 * B_HEADS, tm), lambda i: (0, i)),
        ),
        compiler_params=pltpu.CompilerParams(
            dimension_semantics=("parallel",), vmem_limit_bytes=VMEM_LIMIT),
        name="in_proj_a",
    )(x, gmix, w_uv, w_small, w_small_t, gav)


def _in_proj_b_kernel(h_ref, w_ref, out_ref):
    j = pl.program_id(1)
    acc = _dot(h_ref[...], w_ref[...])
    out_ref[...] = (jax.nn.sigmoid(acc) * jnp.where(j == N_GATE_TILES, acc, 1.0)).astype(BF16)


def _halo_rows(hc, h_ref, hp_ref, hn_ref, i, tiles_per_seq):
    tm = h_ref.shape[0]
    at_start = (i % tiles_per_seq) == 0
    at_end = (i % tiles_per_seq) == tiles_per_seq - 1
    hc[0:HALO, :] = jnp.where(at_start, jnp.zeros_like(hp_ref[...]), hp_ref[...])
    hc[HALO:HALO + tm, :] = h_ref[...]
    hc[HALO + tm:, :] = jnp.where(at_end, jnp.zeros_like(hn_ref[...]), hn_ref[...])


def _conv_rows(up, cw, tm):
    n_rows = tm + 2 * HALO
    prev = pltpu.roll(up, 1, 0)[HALO:HALO + tm]
    nxt = pltpu.roll(up, n_rows - 1, 0)[HALO:HALO + tm]
    return cw[0:1, :] * prev + cw[1:2, :] * up[HALO:HALO + tm] + cw[2:3, :] * nxt


def _halo_specs(tm, t):
    hb = tm // HALO
    last = t // HALO - 1
    return (pl.BlockSpec((HALO, D_MODEL), lambda i, j: (jnp.maximum(i * hb - 1, 0), 0)),
            pl.BlockSpec((HALO, D_MODEL), lambda i, j: (jnp.minimum((i + 1) * hb, last), 0)))


def _in_proj_qkv_kernel(h_ref, hp_ref, hn_ref, w_ref, cw_ref, out_ref, hc, *, tiles_per_seq):
    i = pl.program_id(0)
    j = pl.program_id(1)
    tm = h_ref.shape[0]

    @pl.when(j == 0)
    def _():
        _halo_rows(hc, h_ref, hp_ref, hn_ref, i, tiles_per_seq)

    y = _conv_rows(_dot(hc[...], w_ref[...]), cw_ref[...], tm)
    y = y * jax.nn.sigmoid(y)
    scale = jnp.where(j == 0, Q_SCALE, 1.0)
    for g in range(B_HEADS):
        yg = y[:, g * B_HEAD_DIM:(g + 1) * B_HEAD_DIM]
        inv = lax.rsqrt(jnp.sum(yg * yg, axis=-1, keepdims=True) + EPS) * scale
        out_ref[g] = (yg * jnp.where(j < 2, inv, 1.0)).astype(BF16)


def _in_proj_qkv(h, w_qkv, conv_qkv, seq_len):
    t = h.shape[0]
    tm = PROJ_ROW_TILE
    prev_spec, next_spec = _halo_specs(tm, t)
    return pl.pallas_call(
        functools.partial(_in_proj_qkv_kernel, tiles_per_seq=seq_len // tm),
        out_shape=jax.ShapeDtypeStruct((3 * B_HEADS, t, B_HEAD_DIM), BF16),
        grid=(t // tm, 3),
        in_specs=[
            pl.BlockSpec((tm, D_MODEL), lambda i, j: (i, 0)),
            prev_spec,
            next_spec,
            pl.BlockSpec((D_MODEL, B_WIDTH), lambda i, j: (0, j)),
            pl.BlockSpec((3, B_WIDTH), lambda i, j: (0, j)),
        ],
        out_specs=pl.BlockSpec((B_HEADS, tm, B_HEAD_DIM), lambda i, j: (j, i, 0)),
        scratch_shapes=[pltpu.VMEM((tm + 2 * HALO, D_MODEL), BF16)],
        compiler_params=pltpu.CompilerParams(
            dimension_semantics=("parallel", "arbitrary"), vmem_limit_bytes=VMEM_LIMIT),
        name="in_proj_qkv",
    )(h, h, h, w_qkv, conv_qkv)


def _in_proj_b(h, w_main):
    t = h.shape[0]
    tm, tn = PROJ_ROW_TILE, PROJ_COL_TILE
    return pl.pallas_call(
        _in_proj_b_kernel,
        out_shape=jax.ShapeDtypeStruct((t, N_MAIN), BF16),
        grid=(t // tm, N_MAIN // tn),
        in_specs=[
            pl.BlockSpec((tm, D_MODEL), lambda i, j: (i, 0)),
            pl.BlockSpec((D_MODEL, tn), lambda i, j: (0, j)),
        ],
        out_specs=pl.BlockSpec((tm, tn), lambda i, j: (i, j)),
        compiler_params=pltpu.CompilerParams(
            dimension_semantics=("parallel", "arbitrary"), vmem_limit_bytes=VMEM_LIMIT),
        name="in_proj_b",
    )(h, w_main)


def _softplus(x):
    return jnp.maximum(x, 0.0) + jnp.log1p(jnp.exp(-jnp.abs(x)))


def _gate_prep_kernel(small_ref, smallt_ref, alog_r, dtb_r, alog_c, dtb_c, gbh_ref, grh_ref):
    tb = small_ref.shape[0]
    nl = N_SMALL
    row = lax.broadcasted_iota(jnp.int32, (tb, tb), 0)
    col = lax.broadcasted_iota(jnp.int32, (tb, tb), 1)
    same = (row // B_CHUNK) == (col // B_CHUNK)
    m_lo = jnp.where(same & (col <= row), 1.0, 0.0).astype(BF16)
    m_up = jnp.where(same & (col >= row), 1.0, 0.0).astype(BF16)

    sm = small_ref[:, 0:nl]
    g = -jnp.exp(alog_r[...]) * _softplus(sm + dtb_r[...])
    g3 = jnp.concatenate(_split3(g), axis=1)
    pf = _dot(m_lo, g3)
    pb = _dot(m_up, g3)
    pf = pf[:, 0:nl] + pf[:, nl:2 * nl] + pf[:, 2 * nl:3 * nl]
    pb = pb[:, 0:nl] + pb[:, nl:2 * nl] + pb[:, 2 * nl:3 * nl]
    lane = lax.broadcasted_iota(jnp.int32, (tb, nl), 1) % 4
    vals = jnp.where(lane == 0, pf, jnp.where(lane == 1, pb, jax.nn.sigmoid(sm)))
    for h in range(B_HEADS):
        gbh_ref[h] = vals[:, 4 * h:4 * h + 4]

    gt = -jnp.exp(alog_c[...]) * _softplus(smallt_ref[...] + dtb_c[...])
    gt3 = jnp.concatenate(_split3(gt), axis=0)
    nr = 2 * B_HEADS
    rf = _dot(gt3, m_up)
    rb = _dot(gt3, m_lo)
    rf = rf[0:nr] + rf[nr:2 * nr] + rf[2 * nr:3 * nr]
    rb = rb[0:nr] + rb[nr:2 * nr] + rb[2 * nr:3 * nr]
    sub = lax.broadcasted_iota(jnp.int32, (nr, tb), 0) % 2
    grow = jnp.where(sub == 0, rf, rb)
    for c in range(tb // B_CHUNK):
        for h in range(B_HEADS):
            grh_ref[h, c, :, 0:B_CHUNK] = grow[2 * h:2 * h + 2, c * B_CHUNK:(c + 1) * B_CHUNK]
            grh_ref[h, c, :, B_CHUNK:] = jnp.zeros((2, 128 - B_CHUNK), F32)


def _gate_prep(small, small_t, alog_r, dtb_r, alog_c, dtb_c):
    t = small.shape[0]
    tb = TOKEN_TILE
    nc = tb // B_CHUNK
    return pl.pallas_call(
        _gate_prep_kernel,
        out_shape=(jax.ShapeDtypeStruct((B_HEADS, t, 4), F32),
                   jax.ShapeDtypeStruct((B_HEADS, t // B_CHUNK, 2, 128), F32)),
        grid=(t // tb,),
        in_specs=[
            pl.BlockSpec((tb, 128), lambda i: (i, 0)),
            pl.BlockSpec((2 * B_HEADS, tb), lambda i: (0, i)),
            pl.BlockSpec((1, N_SMALL), lambda i: (0, 0)),
            pl.BlockSpec((1, N_SMALL), lambda i: (0, 0)),
            pl.BlockSpec((2 * B_HEADS, 1), lambda i: (0, 0)),
            pl.BlockSpec((2 * B_HEADS, 1), lambda i: (0, 0)),
        ],
        out_specs=(
            pl.BlockSpec((B_HEADS, tb, 4), lambda i: (0, i, 0)),
            pl.BlockSpec((B_HEADS, nc, 2, 128), lambda i: (0, i, 0, 0)),
        ),
        compiler_params=pltpu.CompilerParams(
            dimension_semantics=("parallel",), vmem_limit_bytes=VMEM_LIMIT),
        name="gate_prep",
    )(small, small_t, alog_r, dtb_r, alog_c, dtb_c)


def _gdn_kernel(q_ref, k_ref, v_ref, gbh_ref, grh_ref, z_ref, gout_ref,
                o_ref, oacc, lhs_s, b_s, ob_s, egl_s, *, seq_len, heads):
    s = seq_len
    c = B_CHUNK
    n_chunks = s // c
    sl = min(GDN_SLAB, s)
    oacc[...] = jnp.zeros_like(oacc)

    rowi = lax.broadcasted_iota(jnp.int32, (c, 2 * c), 0)
    coli = lax.broadcasted_iota(jnp.int32, (c, 2 * c), 1)
    zk = jnp.zeros((c, B_HEAD_DIM), BF16)
    zw = jnp.zeros((c, 2 * B_HEAD_DIM), BF16)
    zw2 = jnp.zeros((c, 4 * c), BF16)

    grp = GDN_GROUP
    n_groups = n_chunks // grp
    per_slot = 2 * grp

    def group_chunks(gi):
        hh = gi // n_groups
        gl_ = gi % n_groups
        out = []
        for g in range(grp):
            out += [(hh, gl_ * grp + g, True), (hh, n_chunks - 1 - (gl_ * grp + g), False)]
        return out

    def prep_stages(gi, slot):
        st = []
        for hh, ci, fwd in group_chunks(gi):
            r0 = pl.multiple_of(ci * c, c)
            q = q_ref[hh, pl.ds(r0, c), :]
            k = k_ref[hh, pl.ds(r0, c), :]
            v = v_ref[hh, pl.ds(r0, c), :]
            gb = gbh_ref[hh, pl.ds(r0, c), :]
            gr = grh_ref[hh, ci]
            if fwd:
                gi_, beta, gj = gb[:, 0:1], gb[:, 2:3], gr[0:1, :]
                gl = gb[c - 1:c, 0:1]
                inc, strict = rowi >= coli, rowi > coli
            else:
                gi_, beta, gj = gb[:, 1:2], gb[:, 3:4], gr[1:2, :]
                gl = gb[0:1, 1:2]
                inc, strict = (rowi <= coli) & (coli < c), (rowi < coli) & (coli < c)
            kf = k.astype(F32)
            kb = kf * beta
            st.append(dict(q=q, k=k, v=v, gi=gi_, beta=beta, gl=gl, inc=inc, strict=strict,
                           kf=kf, kb=kb, eg=jnp.exp(gi_),
                           dec=jnp.exp(jnp.where(inc, gi_ - gj, -1e30))))
        for d in st:
            d["a"] = _dot_nt(jnp.concatenate([d["kb"].astype(BF16), d["q"]], axis=0),
                             jnp.concatenate([d["k"], zk], axis=0))
        yield
        for d in st:
            l_pad = jnp.where(d["strict"], d["a"][0:c] * d["dec"], 0.0)
            d["qk"] = jnp.where(d["inc"], d["a"][c:2 * c] * d["dec"], 0.0).astype(BF16)
            d["px"] = jnp.where(coli == rowi + c, 1.0, 0.0) - l_pad
        for _ in range(int(math.log2(c))):
            for d in st:
                hi = d["px"].astype(BF16)
                lo = (d["px"] - hi.astype(F32)).astype(BF16)
                rhs = jnp.concatenate([jnp.concatenate([hi, lo], axis=1), zw2], axis=0)
                d["o"] = _dot(hi, rhs)
                d["o2"] = _dot(lo, jnp.concatenate([hi, zk], axis=0))
            yield
            for d in st:
                o = d["o"]
                d["px"] = (o[:, :2 * c] + o[:, 2 * c:] + d["o2"]
                           + jnp.where(coli >= c, d["px"], 0.0))
        for d in st:
            rhs = jnp.concatenate([d["kb"] * d["eg"], d["v"].astype(F32) * d["beta"]], axis=1)
            d["wu"] = _dot(d["px"].astype(BF16),
                           jnp.concatenate([zw, rhs.astype(BF16)], axis=0)).astype(BF16)
        yield
        for d in st:
            kd = (d["kf"] * jnp.exp(d["gl"] - d["gi"])).astype(BF16)
            d["m1"] = _dot_tn(kd, d["wu"])
        yield
        for d in st:
            d["m2"] = _dot(d["qk"], jnp.concatenate([d["wu"], zw], axis=0))
        yield
        for j, d in enumerate(st):
            idx = slot * per_slot + j
            qe = d["q"].astype(F32) * d["eg"] - d["m2"][:, :B_HEAD_DIM]
            lhs_s[idx] = jnp.concatenate([-d["m1"][:, :B_HEAD_DIM], qe], axis=0).astype(BF16)
            b_s[idx] = d["m1"][:, B_HEAD_DIM:]
            ob_s[idx] = d["m2"][:, B_HEAD_DIM:]
            egl_s[idx] = jnp.broadcast_to(jnp.exp(d["gl"]), (8, B_HEAD_DIM))

    def scan_step(gi, slot, g, states):
        if g == 0:
            keep = jnp.where(gi % n_groups == 0, 0.0, 1.0)
            states = tuple(st_ * keep for st_ in states)
        new = []
        for (hh, ci, _), state, j in zip(group_chunks(gi)[2 * g:2 * g + 2], states,
                                          (2 * g, 2 * g + 1)):
            idx = slot * per_slot + j
            rows = pl.ds(pl.multiple_of(hh * s + ci * c, c), c)
            res = _dot(lhs_s[idx], state.astype(BF16))
            oacc[rows, :] += res[B_HEAD_DIM:] + ob_s[idx]
            new.append(egl_s[idx][0:1, :] * state + res[:B_HEAD_DIM] + b_s[idx])
        return tuple(new)

    for _ in prep_stages(0, 0):
        pass

    def step(i, states):
        slot = i % 2
        g = 0
        for k_stage, _ in enumerate(prep_stages(i + 1, 1 - slot)):
            if k_stage % 2 == 0 and g < grp:
                states = scan_step(i, slot, g, states)
                g += 1
        assert g == grp
        return states

    z0 = jnp.zeros((B_HEAD_DIM, B_HEAD_DIM), F32)
    last = heads * n_groups - 1
    states = lax.fori_loop(0, last, step, (z0, z0))
    for g in range(grp):
        states = scan_step(last, last % 2, g, states)

    def post(i, carry, hh):
        r0 = pl.multiple_of(i * sl, sl)
        cols = slice(hh * B_HEAD_DIM, (hh + 1) * B_HEAD_DIM)
        y = (_rms(oacc[pl.ds(hh * s + r0, sl), :], gout_ref[...])
             * z_ref[pl.ds(r0, sl), cols].astype(F32))
        o_ref[pl.ds(r0, sl), cols] = y.astype(BF16)
        return carry

    for hh in range(heads):
        lax.fori_loop(0, s // sl, functools.partial(post, hh=hh), 0)


def _gdn_heads_per_step(seq_len):
    per_head = seq_len * B_HEAD_DIM * (
        3 * 2 * 2
        + 2 * 2 * 2
        + 4 * 2
        + 4)
    heads = B_HEADS
    while heads > 1 and heads * per_head > VMEM_LIMIT * 3 // 4:
        heads //= 2
    return heads


def _gdn(qkv, main, gbh, grh, gout, seq_len):
    t = main.shape[0]
    s = seq_len
    hd = B_HEAD_DIM
    hps = _gdn_heads_per_step(s)
    wd = hps * hd
    nhb = B_HEADS // hps
    z0 = 2 * D_MODEL // wd
    kern = functools.partial(_gdn_kernel, seq_len=s, heads=hps)
    n_slots = 2 * 2 * GDN_GROUP
    return pl.pallas_call(
        kern,
        out_shape=jax.ShapeDtypeStruct((t, B_WIDTH), BF16),
        grid=(t // s, nhb),
        in_specs=[
            pl.BlockSpec((hps, s, hd), lambda n, h: (h, n, 0)),
            pl.BlockSpec((hps, s, hd), lambda n, h: (nhb + h, n, 0)),
            pl.BlockSpec((hps, s, hd), lambda n, h: (2 * nhb + h, n, 0)),
            pl.BlockSpec((hps, s, 4), lambda n, h: (h, n, 0)),
            pl.BlockSpec((hps, s // B_CHUNK, 2, 128), lambda n, h: (h, n, 0, 0)),
            pl.BlockSpec((s, wd), lambda n, h: (n, z0 + h)),
            pl.BlockSpec((1, hd), lambda n, h: (0, 0)),
        ],
        out_specs=pl.BlockSpec((s, wd), lambda n, h: (n, h)),
        scratch_shapes=[pltpu.VMEM((hps * s, hd), F32),
                        pltpu.VMEM((n_slots, hd + B_CHUNK, hd), BF16),
                        pltpu.VMEM((n_slots, hd, hd), F32),
                        pltpu.VMEM((n_slots, B_CHUNK, hd), F32),
                        pltpu.VMEM((n_slots, 8, hd), F32)],
        compiler_params=pltpu.CompilerParams(
            dimension_semantics=("parallel", "parallel"), vmem_limit_bytes=VMEM_LIMIT),
        name="gdn",
    )(qkv, qkv, qkv, gbh, grh, main, gout)


def _merge_kernel(u_ref, v_ref, ws_ref, bst_ref, b_ref, ga_ref, gb_ref, x_ref,
                  wa_ref, wb_ref, wo_ref, gffn_ref, x1_ref, h2_ref, a_s):
    tm = u_ref.shape[0]
    gd = A_WIDTH // A_GROUPS
    for ci in range(tm // A_CHUNK):
        rows = slice(ci * A_CHUNK, (ci + 1) * A_CHUNK)
        for g in range(A_GROUPS):
            cols = slice(g * gd, (g + 1) * gd)
            mixed = _dot(ws_ref[g], v_ref[rows, cols]) + bst_ref[:, g:g + 1]
            a_s[rows, cols] = (u_ref[rows, cols].astype(F32) * mixed).astype(BF16)
    pa = _dot(a_s[...], wa_ref[...])
    pb = _dot(b_ref[...], wb_ref[...])
    merged = ga_ref[...].astype(F32) * pa + gb_ref[...].astype(F32) * pb
    x1 = x_ref[...] + _dot(merged.astype(BF16), wo_ref[...])
    x1_ref[...] = x1
    h2_ref[...] = _rms(x1, gffn_ref[...]).astype(BF16)


def _merge(uv, main, b, x, ws, bst, wa, wb, wo, gffn):
    t = x.shape[0]
    tm = TOKEN_TILE
    gate0 = 0
    const = dict(pipeline_mode=pl.Buffered(1))
    return pl.pallas_call(
        _merge_kernel,
        out_shape=(jax.ShapeDtypeStruct((t, D_MODEL), F32),
                   jax.ShapeDtypeStruct((t, D_MODEL), BF16)),
        grid=(t // tm,),
        in_specs=[
            pl.BlockSpec((tm, A_WIDTH), lambda i: (i, 0)),
            pl.BlockSpec((tm, A_WIDTH), lambda i: (i, 1)),
            pl.BlockSpec((A_GROUPS, A_CHUNK, A_CHUNK), lambda i: (0, 0, 0), **const),
            pl.BlockSpec((A_CHUNK, A_GROUPS), lambda i: (0, 0), **const),
            pl.BlockSpec((tm, B_WIDTH), lambda i: (i, 0)),
            pl.BlockSpec((tm, D_MODEL), lambda i: (i, gate0)),
            pl.BlockSpec((tm, D_MODEL), lambda i: (i, gate0 + 1)),
            pl.BlockSpec((tm, D_MODEL), lambda i: (i, 0)),
            pl.BlockSpec((A_WIDTH, D_MODEL), lambda i: (0, 0), **const),
            pl.BlockSpec((B_WIDTH, D_MODEL), lambda i: (0, 0), **const),
            pl.BlockSpec((D_MODEL, D_MODEL), lambda i: (0, 0), **const),
            pl.BlockSpec((1, D_MODEL), lambda i: (0, 0), **const),
        ],
        out_specs=(
            pl.BlockSpec((tm, D_MODEL), lambda i: (i, 0)),
            pl.BlockSpec((tm, D_MODEL), lambda i: (i, 0)),
        ),
        scratch_shapes=[pltpu.VMEM((tm, A_WIDTH), BF16)],
        compiler_params=pltpu.CompilerParams(
            dimension_semantics=("parallel",), vmem_limit_bytes=VMEM_LIMIT),
        name="merge",
    )(uv, uv, ws, bst, b, main, main, x, wa, wb, wo, gffn)


def _ffn_kernel(h_ref, hp_ref, hn_ref, x1_ref, w1_ref, w2_ref, c1_ref, c2_ref, wd_ref,
                x2_ref, hc, acc, *, tiles_per_seq):
    i = pl.program_id(0)
    j = pl.program_id(1)
    tm = h_ref.shape[0]

    @pl.when(j == 0)
    def _():
        _halo_rows(hc, h_ref, hp_ref, hn_ref, i, tiles_per_seq)
        acc[...] = jnp.zeros_like(acc)

    lhs = hc[...]
    a1 = _conv_rows(_dot(lhs, w1_ref[...]), c1_ref[...], tm)
    a2 = _conv_rows(_dot(lhs, w2_ref[...]), c2_ref[...], tm)
    act = (a1 * jax.nn.sigmoid(a1) * a2).astype(BF16)
    acc[...] += _dot(act, wd_ref[...])

    @pl.when(j == pl.num_programs(1) - 1)
    def _():
        x2_ref[...] = x1_ref[...] + acc[...]


def _ffn(h2, x1, w_up, conv_ffn, w_down, seq_len):
    t = h2.shape[0]
    tm, tf = TOKEN_TILE, FFN_COL_TILE
    nj = D_FF // tf
    prev_spec, next_spec = _halo_specs(tm, t)
    kern = functools.partial(_ffn_kernel, tiles_per_seq=seq_len // tm)
    return pl.pallas_call(
        kern,
        out_shape=jax.ShapeDtypeStruct((t, D_MODEL), F32),
        grid=(t // tm, nj),
        in_specs=[
            pl.BlockSpec((tm, D_MODEL), lambda i, j: (i, 0)),
            prev_spec,
            next_spec,
            pl.BlockSpec((tm, D_MODEL), lambda i, j: (i, 0)),
            pl.BlockSpec((D_MODEL, tf), lambda i, j: (0, j)),
            pl.BlockSpec((D_MODEL, tf), lambda i, j: (0, nj + j)),
            pl.BlockSpec((3, tf), lambda i, j: (0, j)),
            pl.BlockSpec((3, tf), lambda i, j: (0, nj + j)),
            pl.BlockSpec((tf, D_MODEL), lambda i, j: (j, 0)),
        ],
        out_specs=pl.BlockSpec((tm, D_MODEL), lambda i, j: (i, 0)),
        scratch_shapes=[pltpu.VMEM((tm + 2 * HALO, D_MODEL), BF16),
                        pltpu.VMEM((tm, D_MODEL), F32)],
        compiler_params=pltpu.CompilerParams(
            dimension_semantics=("parallel", "arbitrary"), vmem_limit_bytes=VMEM_LIMIT),
        name="ffn",
    )(h2, h2, h2, x1, w_up, w_up, conv_ffn, conv_ffn, w_down)


def _ple_final_kernel(x2_ref, p_ref, gple_ref, wg_ref, wp_ref, gfin_ref, y_ref, *, final):
    x2 = x2_ref[...]
    gate = jax.nn.sigmoid(_dot(_rms(x2, gple_ref[...]).astype(BF16), wg_ref[...]))
    x3 = x2 + gate * _dot(p_ref[...].astype(BF16), wp_ref[...])
    y_ref[...] = _rms(x3, gfin_ref[...]) if final else x3


def _ple_final(x2, p, gple, wg, wp, gfin, final):
    t = x2.shape[0]
    tm = TOKEN_TILE
    const = dict(pipeline_mode=pl.Buffered(1))
    return pl.pallas_call(
        functools.partial(_ple_final_kernel, final=final),
        out_shape=jax.ShapeDtypeStruct((t, D_MODEL), F32),
        grid=(t // tm,),
        in_specs=[
            pl.BlockSpec((tm, D_MODEL), lambda i: (i, 0)),
            pl.BlockSpec((tm, PLE_DIM), lambda i: (i, 0)),
            pl.BlockSpec((1, D_MODEL), lambda i: (0, 0), **const),
            pl.BlockSpec((D_MODEL, D_MODEL), lambda i: (0, 0), **const),
            pl.BlockSpec((PLE_DIM, D_MODEL), lambda i: (0, 0), **const),
            pl.BlockSpec((1, D_MODEL), lambda i: (0, 0), **const),
        ],
        out_specs=pl.BlockSpec((tm, D_MODEL), lambda i: (i, 0)),
        compiler_params=pltpu.CompilerParams(
            dimension_semantics=("parallel",), vmem_limit_bytes=VMEM_LIMIT),
        name="ple_final",
    )(x2, p, gple, wg, wp, gfin)


def _prep_layer(g_mix, w_in, g_a_v, w_s, b_s, conv_qkv, a_log_f, a_log_b, dt_bias_f, dt_bias_b,
                g_b_out, w_a_out, w_b_out, w_out, g_ffn, w_up, conv_ffn, w_down, g_ple,
                w_ple_gate, w_ple):
    n0 = 2 * A_WIDTH + 4 * B_WIDTH
    w_uv = w_in[:, :2 * A_WIDTH].astype(BF16)
    w_qkv = w_in[:, 2 * A_WIDTH:2 * A_WIDTH + 3 * B_WIDTH].astype(BF16)
    w_main = jnp.concatenate([w_in[:, n0 + N_SMALL:], w_in[:, n0 - B_WIDTH:n0]], axis=1).astype(BF16)
    sm = w_in[:, n0:n0 + N_SMALL].reshape(D_MODEL, 4, B_HEADS)
    per_head = jnp.stack([sm[:, 2], sm[:, 3], sm[:, 0], sm[:, 1]], axis=-1)
    w_small = jnp.pad(per_head.reshape(D_MODEL, N_SMALL), ((0, 0), (0, 128 - N_SMALL))).astype(BF16)
    w_small_t = jnp.stack([sm[:, 2], sm[:, 3]], axis=-1).reshape(D_MODEL, 2 * B_HEADS).T.astype(BF16)
    zeros = jnp.zeros((B_HEADS,), F32)
    alog_r = jnp.stack([a_log_f, a_log_b, zeros, zeros], axis=-1).reshape(1, N_SMALL)
    dtb_r = jnp.stack([dt_bias_f, dt_bias_b, zeros, zeros], axis=-1).reshape(1, N_SMALL)
    alog_c = jnp.stack([a_log_f, a_log_b], axis=-1).reshape(2 * B_HEADS, 1)
    dtb_c = jnp.stack([dt_bias_f, dt_bias_b], axis=-1).reshape(2 * B_HEADS, 1)
    return dict(
        g_mix=g_mix.reshape(1, D_MODEL), w_uv=w_uv, w_qkv=w_qkv, w_main=w_main, w_small=w_small,
        w_small_t=w_small_t,
        g_a_v=g_a_v.reshape(1, A_WIDTH), w_s=w_s.astype(BF16), bst=b_s.T,
        conv_qkv=conv_qkv, alog_r=alog_r, dtb_r=dtb_r, alog_c=alog_c, dtb_c=dtb_c,
        g_b_out=g_b_out.reshape(1, B_HEAD_DIM), w_a_out=w_a_out.astype(BF16),
        w_b_out=w_b_out.astype(BF16), w_out=w_out.astype(BF16), g_ffn=g_ffn.reshape(1, D_MODEL),
        w_up=w_up.astype(BF16), conv_ffn=conv_ffn, w_down=w_down.astype(BF16),
        g_ple=g_ple.reshape(1, D_MODEL), w_ple_gate=w_ple_gate.astype(BF16),
        w_ple=w_ple.astype(BF16))


def _layer(x, p, w, gfin, seq_len, final):
    h, uv, small, small_t = _in_proj_a(x, w["g_mix"], w["w_uv"], w["w_small"], w["w_small_t"],
                                       w["g_a_v"])
    qkv = _in_proj_qkv(h, w["w_qkv"], w["conv_qkv"], seq_len)
    main = _in_proj_b(h, w["w_main"])
    gbh, grh = _gate_prep(small, small_t, w["alog_r"], w["dtb_r"], w["alog_c"], w["dtb_c"])
    b = _gdn(qkv, main, gbh, grh, w["g_b_out"], seq_len)
    x1, h2 = _merge(uv, main, b, x, w["w_s"], w["bst"], w["w_a_out"], w["w_b_out"], w["w_out"],
                    w["g_ffn"])
    x2 = _ffn(h2, x1, w["w_up"], w["conv_ffn"], w["w_down"], seq_len)
    return _ple_final(x2, p, w["g_ple"], w["w_ple_gate"], w["w_ple"], gfin, final)


def kernel(x_prompt, x_sample, p_prompt, p_sample, g_mix, w_in, g_a_v, w_s, b_s, conv_qkv, a_log_f, a_log_b, dt_bias_f, dt_bias_b, g_b_out, w_a_out, w_b_out, w_out, g_ffn, w_up, conv_ffn, w_down, g_ple, w_ple_gate, w_ple, g_final):
    depth = w_in.shape[0]
    per_layer = (g_mix, w_in, g_a_v, w_s, b_s, conv_qkv, a_log_f, a_log_b, dt_bias_f, dt_bias_b,
                 g_b_out, w_a_out, w_b_out, w_out, g_ffn, w_up, conv_ffn, w_down, g_ple,
                 w_ple_gate, w_ple)
    layers = [_prep_layer(*(a[i] for a in per_layer)) for i in range(depth)]
    gfin = g_final.reshape(1, D_MODEL)

    def trunk(x, p):
        bsz, seq, _ = x.shape
        xf = x.reshape(bsz * seq, D_MODEL)
        for i, w in enumerate(layers):
            xf = _layer(xf, p[i].reshape(bsz * seq, PLE_DIM), w, gfin, seq, i == depth - 1)
        return xf.reshape(bsz, seq, D_MODEL)

    return (trunk(x_prompt, p_prompt), trunk(x_sample, p_sample))
```

```python
import functools
import math

import jax
import jax.numpy as jnp
from jax import lax
from jax.experimental import pallas as pl
from jax.experimental.pallas import tpu as pltpu

F32 = jnp.float32
BF16 = jnp.bfloat16

D_MODEL = 2048
PLE_DIM = 256
A_WIDTH = 1024
A_GROUPS = 8
A_CHUNK = 128
B_HEADS = 8
B_HEAD_DIM = 128
B_WIDTH = B_HEADS * B_HEAD_DIM
B_CHUNK = 64
Q_SCALE = 1.0 / math.sqrt(B_HEAD_DIM)
D_FF = 4096
EPS = 1e-6

N_MAIN = 2 * D_MODEL + B_WIDTH
N_SMALL = 4 * B_HEADS

TOKEN_TILE = 512
PROJ_ROW_TILE = 1024
PROJ_COL_TILE = 1024
N_GATE_TILES = 2 * D_MODEL // PROJ_COL_TILE
FFN_COL_TILE = 1024
HALO = 16
GDN_SLAB = 256
GDN_GROUP = 4
VMEM_LIMIT = 60000 * 1024


def _rms(x, g):
    return x * lax.rsqrt(jnp.mean(x * x, axis=-1, keepdims=True) + EPS) * g


def _gelu(x):
    return 0.5 * x * (1.0 + lax.erf(x * math.sqrt(0.5)))


def _dot(a, b):
    return jnp.dot(a, b, preferred_element_type=F32)


def _dot_nt(a, b):
    return lax.dot_general(a, b, (((1,), (1,)), ((), ())), preferred_element_type=F32)


def _dot_tn(a, b):
    return lax.dot_general(a, b, (((0,), (0,)), ((), ())), preferred_element_type=F32)


def _split3(x):
    hi = x.astype(BF16)
    r1 = x - hi.astype(F32)
    mid = r1.astype(BF16)
    lo = (r1 - mid.astype(F32)).astype(BF16)
    return hi, mid, lo


def _in_proj_a_kernel(x_ref, gmix_ref, w_ref, wsm_ref, wsmt_ref, gav_ref,
                      h_ref, uv_ref, small_ref, smallt_ref):
    h = _rms(x_ref[...], gmix_ref[...]).astype(BF16)
    h_ref[...] = h
    small_ref[...] = _dot(h, wsm_ref[...])
    smallt_ref[...] = _dot_nt(wsmt_ref[...], h)
    uv_ref[:, :A_WIDTH] = _gelu(_dot(h, w_ref[:, :A_WIDTH])).astype(BF16)
    v = _gelu(_dot(h, w_ref[:, A_WIDTH:]))
    uv_ref[:, A_WIDTH:] = _rms(v, gav_ref[...]).astype(BF16)


def _in_proj_a(x, gmix, w_uv, w_small, w_small_t, gav):
    t = x.shape[0]
    tm = TOKEN_TILE
    const = dict(pipeline_mode=pl.Buffered(1))
    return pl.pallas_call(
        _in_proj_a_kernel,
        out_shape=(jax.ShapeDtypeStruct((t, D_MODEL), BF16),
                   jax.ShapeDtypeStruct((t, 2 * A_WIDTH), BF16),
                   jax.ShapeDtypeStruct((t, 128), F32),
                   jax.ShapeDtypeStruct((2 * B_HEADS, t), F32)),
        grid=(t // tm,),
        in_specs=[
            pl.BlockSpec((tm, D_MODEL), lambda i: (i, 0)),
            pl.BlockSpec((1, D_MODEL), lambda i: (0, 0), **const),
            pl.BlockSpec((D_MODEL, 2 * A_WIDTH), lambda i: (0, 0), **const),
            pl.BlockSpec((D_MODEL, 128), lambda i: (0, 0), **const),
            pl.BlockSpec((2 * B_HEADS, D_MODEL), lambda i: (0, 0), **const),
            pl.BlockSpec((1, A_WIDTH), lambda i: (0, 0), **const),
        ],
        out_specs=(
            pl.BlockSpec((tm, D_MODEL), lambda i: (i, 0)),
            pl.BlockSpec((tm, 2 * A_WIDTH), lambda i: (i, 0)),
            pl.BlockSpec((tm, 128), lambda i: (i, 0)),
            pl.BlockSpec((2 * B_HEADS, tm), lambda i: (0, i)),
        ),
        compiler_params=pltpu.CompilerParams(
            dimension_semantics=("parallel",), vmem_limit_bytes=VMEM_LIMIT),
        name="in_proj_a",
    )(x, gmix, w_uv, w_small, w_small_t, gav)


def _in_proj_b_kernel(h_ref, w_ref, out_ref):
    j = pl.program_id(1)
    acc = _dot(h_ref[...], w_ref[...])
    out_ref[...] = (jax.nn.sigmoid(acc) * jnp.where(j == N_GATE_TILES, acc, 1.0)).astype(BF16)


def _halo_rows(hc, h_ref, hp_ref, hn_ref, i, tiles_per_seq):
    tm = h_ref.shape[0]
    at_start = (i % tiles_per_seq) == 0
    at_end = (i % tiles_per_seq) == tiles_per_seq - 1
    hc[0:HALO, :] = jnp.where(at_start, jnp.zeros_like(hp_ref[...]), hp_ref[...])
    hc[HALO:HALO + tm, :] = h_ref[...]
    hc[HALO + tm:, :] = jnp.where(at_end, jnp.zeros_like(hn_ref[...]), hn_ref[...])


def _conv_rows(up, cw, tm):
    n_rows = tm + 2 * HALO
    prev = pltpu.roll(up, 1, 0)[HALO:HALO + tm]
    nxt = pltpu.roll(up, n_rows - 1, 0)[HALO:HALO + tm]
    return cw[0:1, :] * prev + cw[1:2, :] * up[HALO:HALO + tm] + cw[2:3, :] * nxt


def _halo_specs(tm, t):
    hb = tm // HALO
    last = t // HALO - 1
    return (pl.BlockSpec((HALO, D_MODEL), lambda i, j: (jnp.maximum(i * hb - 1, 0), 0)),
            pl.BlockSpec((HALO, D_MODEL), lambda i, j: (jnp.minimum((i + 1) * hb, last), 0)))


def _in_proj_qkv_kernel(h_ref, hp_ref, hn_ref, w_ref, cw_ref, out_ref, hc, *, tiles_per_seq):
    i = pl.program_id(0)
    j = pl.program_id(1)
    tm = h_ref.shape[0]

    @pl.when(j == 0)
    def _():
        _halo_rows(hc, h_ref, hp_ref, hn_ref, i, tiles_per_seq)

    y = _conv_rows(_dot(hc[...], w_ref[...]), cw_ref[...], tm)
    y = y * jax.nn.sigmoid(y)
    scale = jnp.where(j == 0, Q_SCALE, 1.0)
    for g in range(B_HEADS):
        yg = y[:, g * B_HEAD_DIM:(g + 1) * B_HEAD_DIM]
        inv = lax.rsqrt(jnp.sum(yg * yg, axis=-1, keepdims=True) + EPS) * scale
        out_ref[g] = (yg * jnp.where(j < 2, inv, 1.0)).astype(BF16)


def _in_proj_qkv(h, w_qkv, conv_qkv, seq_len):
    t = h.shape[0]
    tm = PROJ_ROW_TILE
    prev_spec, next_spec = _halo_specs(tm, t)
    return pl.pallas_call(
        functools.partial(_in_proj_qkv_kernel, tiles_per_seq=seq_len // tm),
        out_shape=jax.ShapeDtypeStruct((3 * B_HEADS, t, B_HEAD_DIM), BF16),
        grid=(t // tm, 3),
        in_specs=[
            pl.BlockSpec((tm, D_MODEL), lambda i, j: (i, 0)),
            prev_spec,
            next_spec,
            pl.BlockSpec((D_MODEL, B_WIDTH), lambda i, j: (0, j)),
            pl.BlockSpec((3, B_WIDTH), lambda i, j: (0, j)),
        ],
        out_specs=pl.BlockSpec((B_HEADS, tm, B_HEAD_DIM), lambda i, j: (j, i, 0)),
        scratch_shapes=[pltpu.VMEM((tm + 2 * HALO, D_MODEL), BF16)],
        compiler_params=pltpu.CompilerParams(
            dimension_semantics=("parallel", "arbitrary"), vmem_limit_bytes=VMEM_LIMIT),
        name="in_proj_qkv",
    )(h, h, h, w_qkv, conv_qkv)


def _in_proj_b(h, w_main):
    t = h.shape[0]
    tm, tn = PROJ_ROW_TILE, PROJ_COL_TILE
    return pl.pallas_call(
        _in_proj_b_kernel,
        out_shape=jax.ShapeDtypeStruct((t, N_MAIN), BF16),
        grid=(t // tm, N_MAIN // tn),
        in_specs=[
            pl.BlockSpec((tm, D_MODEL), lambda i, j: (i, 0)),
            pl.BlockSpec((D_MODEL, tn), lambda i, j: (0, j)),
        ],
        out_specs=pl.BlockSpec((tm, tn), lambda i, j: (i, j)),
        compiler_params=pltpu.CompilerParams(
            dimension_semantics=("parallel", "arbitrary"), vmem_limit_bytes=VMEM_LIMIT),
        name="in_proj_b",
    )(h, w_main)


def _softplus(x):
    return jnp.maximum(x, 0.0) + jnp.log1p(jnp.exp(-jnp.abs(x)))


def _gate_prep_kernel(small_ref, smallt_ref, alog_r, dtb_r, alog_c, dtb_c, gbh_ref, grh_ref):
    tb, nl = small_ref.shape
    row = lax.broadcasted_iota(jnp.int32, (tb, tb), 0)
    col = lax.broadcasted_iota(jnp.int32, (tb, tb), 1)
    same = (row // B_CHUNK) == (col // B_CHUNK)
    m_lo = jnp.where(same & (col <= row), 1.0, 0.0).astype(BF16)
    m_up = jnp.where(same & (col >= row), 1.0, 0.0).astype(BF16)

    sm = small_ref[...]
    g = -jnp.exp(alog_r[...]) * _softplus(sm + dtb_r[...])
    g3 = jnp.concatenate(_split3(g), axis=1)
    pf = _dot(m_lo, g3)
    pb = _dot(m_up, g3)
    pf = pf[:, 0:nl] + pf[:, nl:2 * nl] + pf[:, 2 * nl:3 * nl]
    pb = pb[:, 0:nl] + pb[:, nl:2 * nl] + pb[:, 2 * nl:3 * nl]
    lane = lax.broadcasted_iota(jnp.int32, (tb, nl), 1) % 4
    vals = jnp.where(lane == 0, pf, jnp.where(lane == 1, pb, jax.nn.sigmoid(sm)))
    for h in range(B_HEADS):
        gbh_ref[h] = vals if h == 0 else pltpu.roll(vals, nl - 4 * h, 1)

    gt = -jnp.exp(alog_c[...]) * _softplus(smallt_ref[...] + dtb_c[...])
    gt3 = jnp.concatenate(_split3(gt), axis=0)
    nr = 2 * B_HEADS
    rf = _dot(gt3, m_up)
    rb = _dot(gt3, m_lo)
    rf = rf[0:nr] + rf[nr:2 * nr] + rf[2 * nr:3 * nr]
    rb = rb[0:nr] + rb[nr:2 * nr] + rb[2 * nr:3 * nr]
    sub = lax.broadcasted_iota(jnp.int32, (nr, tb), 0) % 2
    grow = jnp.where(sub == 0, rf, rb)
    for c in range(tb // B_CHUNK):
        for h in range(B_HEADS):
            grh_ref[h, c, :, 0:B_CHUNK] = grow[2 * h:2 * h + 2, c * B_CHUNK:(c + 1) * B_CHUNK]
            grh_ref[h, c, :, B_CHUNK:] = jnp.zeros((2, 128 - B_CHUNK), F32)


def _gate_prep(small, small_t, alog_r, dtb_r, alog_c, dtb_c):
    t = small.shape[0]
    tb = TOKEN_TILE
    nc = tb // B_CHUNK
    return pl.pallas_call(
        _gate_prep_kernel,
        out_shape=(jax.ShapeDtypeStruct((B_HEADS, t, 128), F32),
                   jax.ShapeDtypeStruct((B_HEADS, t // B_CHUNK, 2, 128), F32)),
        grid=(t // tb,),
        in_specs=[
            pl.BlockSpec((tb, 128), lambda i: (i, 0)),
            pl.BlockSpec((2 * B_HEADS, tb), lambda i: (0, i)),
            pl.BlockSpec((1, 128), lambda i: (0, 0)),
            pl.BlockSpec((1, 128), lambda i: (0, 0)),
            pl.BlockSpec((2 * B_HEADS, 1), lambda i: (0, 0)),
            pl.BlockSpec((2 * B_HEADS, 1), lambda i: (0, 0)),
        ],
        out_specs=(
            pl.BlockSpec((B_HEADS, tb, 128), lambda i: (0, i, 0)),
            pl.BlockSpec((B_HEADS, nc, 2, 128), lambda i: (0, i, 0, 0)),
        ),
        compiler_params=pltpu.CompilerParams(
            dimension_semantics=("parallel",), vmem_limit_bytes=VMEM_LIMIT),
        name="gate_prep",
    )(small, small_t, alog_r, dtb_r, alog_c, dtb_c)


def _gdn_kernel(q_ref, k_ref, v_ref, gbh_ref, grh_ref, z_ref, gout_ref,
                o_ref, oacc, lhs_s, b_s, ob_s, egl_s, *, seq_len, heads):
    s = seq_len
    c = B_CHUNK
    n_chunks = s // c
    sl = min(GDN_SLAB, s)
    oacc[...] = jnp.zeros_like(oacc)

    rowi = lax.broadcasted_iota(jnp.int32, (c, 2 * c), 0)
    coli = lax.broadcasted_iota(jnp.int32, (c, 2 * c), 1)
    zk = jnp.zeros((c, B_HEAD_DIM), BF16)
    zw = jnp.zeros((c, 2 * B_HEAD_DIM), BF16)
    zw2 = jnp.zeros((c, 4 * c), BF16)

    grp = GDN_GROUP
    n_groups = n_chunks // grp
    per_slot = 2 * grp

    def group_chunks(gi):
        hh = gi // n_groups
        gl_ = gi % n_groups
        out = []
        for g in range(grp):
            out += [(hh, gl_ * grp + g, True), (hh, n_chunks - 1 - (gl_ * grp + g), False)]
        return out

    def prep_stages(gi, slot):
        st = []
        for hh, ci, fwd in group_chunks(gi):
            r0 = pl.multiple_of(ci * c, c)
            q = q_ref[hh, pl.ds(r0, c), :]
            k = k_ref[hh, pl.ds(r0, c), :]
            v = v_ref[hh, pl.ds(r0, c), :]
            gb = gbh_ref[hh, pl.ds(r0, c), :]
            gr = grh_ref[hh, ci]
            if fwd:
                gi_, beta, gj = gb[:, 0:1], gb[:, 2:3], gr[0:1, :]
                gl = gb[c - 1:c, 0:1]
                inc, strict = rowi >= coli, rowi > coli
            else:
                gi_, beta, gj = gb[:, 1:2], gb[:, 3:4], gr[1:2, :]
                gl = gb[0:1, 1:2]
                inc, strict = (rowi <= coli) & (coli < c), (rowi < coli) & (coli < c)
            kf = k.astype(F32)
            kb = kf * beta
            st.append(dict(q=q, k=k, v=v, gi=gi_, beta=beta, gl=gl, inc=inc, strict=strict,
                           kf=kf, kb=kb, eg=jnp.exp(gi_),
                           dec=jnp.exp(jnp.where(inc, gi_ - gj, -1e30))))
        for d in st:
            d["a"] = _dot_nt(jnp.concatenate([d["kb"].astype(BF16), d["q"]], axis=0),
                             jnp.concatenate([d["k"], zk], axis=0))
        yield
        for d in st:
            l_pad = jnp.where(d["strict"], d["a"][0:c] * d["dec"], 0.0)
            d["qk"] = jnp.where(d["inc"], d["a"][c:2 * c] * d["dec"], 0.0).astype(BF16)
            d["px"] = jnp.where(coli == rowi + c, 1.0, 0.0) - l_pad
        for _ in range(int(math.log2(c))):
            for d in st:
                hi = d["px"].astype(BF16)
                lo = (d["px"] - hi.astype(F32)).astype(BF16)
                rhs = jnp.concatenate([jnp.concatenate([hi, lo], axis=1), zw2], axis=0)
                d["o"] = _dot(hi, rhs)
                d["o2"] = _dot(lo, jnp.concatenate([hi, zk], axis=0))
            yield
            for d in st:
                o = d["o"]
                d["px"] = (o[:, :2 * c] + o[:, 2 * c:] + d["o2"]
                           + jnp.where(coli >= c, d["px"], 0.0))
        for d in st:
            rhs = jnp.concatenate([d["kb"] * d["eg"], d["v"].astype(F32) * d["beta"]], axis=1)
            d["wu"] = _dot(d["px"].astype(BF16),
                           jnp.concatenate([zw, rhs.astype(BF16)], axis=0)).astype(BF16)
        yield
        for d in st:
            kd = (d["kf"] * jnp.exp(d["gl"] - d["gi"])).astype(BF16)
            d["m1"] = _dot_tn(kd, d["wu"])
        yield
        for d in st:
            d["m2"] = _dot(d["qk"], jnp.concatenate([d["wu"], zw], axis=0))
        yield
        for j, d in enumerate(st):
            idx = slot * per_slot + j
            qe = d["q"].astype(F32) * d["eg"] - d["m2"][:, :B_HEAD_DIM]
            lhs_s[idx] = jnp.concatenate([-d["m1"][:, :B_HEAD_DIM], qe], axis=0).astype(BF16)
            b_s[idx] = d["m1"][:, B_HEAD_DIM:]
            ob_s[idx] = d["m2"][:, B_HEAD_DIM:]
            egl_s[idx] = jnp.broadcast_to(jnp.exp(d["gl"]), (8, B_HEAD_DIM))

    def scan_step(gi, slot, g, states):
        if g == 0:
            keep = jnp.where(gi % n_groups == 0, 0.0, 1.0)
            states = tuple(st_ * keep for st_ in states)
        new = []
        for (hh, ci, _), state, j in zip(group_chunks(gi)[2 * g:2 * g + 2], states,
                                          (2 * g, 2 * g + 1)):
            idx = slot * per_slot + j
            rows = pl.ds(pl.multiple_of(hh * s + ci * c, c), c)
            res = _dot(lhs_s[idx], state.astype(BF16))
            oacc[rows, :] += res[B_HEAD_DIM:] + ob_s[idx]
            new.append(egl_s[idx][0:1, :] * state + res[:B_HEAD_DIM] + b_s[idx])
        return tuple(new)

    for _ in prep_stages(0, 0):
        pass

    def step(i, states):
        slot = i % 2
        g = 0
        for k_stage, _ in enumerate(prep_stages(i + 1, 1 - slot)):
            if k_stage % 2 == 0 and g < grp:
                states = scan_step(i, slot, g, states)
                g += 1
        assert g == grp
        return states

    z0 = jnp.zeros((B_HEAD_DIM, B_HEAD_DIM), F32)
    last = heads * n_groups - 1
    states = lax.fori_loop(0, last, step, (z0, z0))
    for g in range(grp):
        states = scan_step(last, last % 2, g, states)

    def post(i, carry, hh):
        r0 = pl.multiple_of(i * sl, sl)
        cols = slice(hh * B_HEAD_DIM, (hh + 1) * B_HEAD_DIM)
        y = (_rms(oacc[pl.ds(hh * s + r0, sl), :], gout_ref[...])
             * z_ref[pl.ds(r0, sl), cols].astype(F32))
        o_ref[pl.ds(r0, sl), cols] = y.astype(BF16)
        return carry

    for hh in range(heads):
        lax.fori_loop(0, s // sl, functools.partial(post, hh=hh), 0)


def _gdn_heads_per_step(seq_len):
    per_head = seq_len * B_HEAD_DIM * (
        3 * 2 * 2
        + 2 * 2 * 2
        + 4 * 2
        + 4)
    heads = B_HEADS
    while heads > 1 and heads * per_head > VMEM_LIMIT * 3 // 4:
        heads //= 2
    return heads


def _gdn(qkv, main, gbh, grh, gout, seq_len):
    t = main.shape[0]
    s = seq_len
    hd = B_HEAD_DIM
    hps = _gdn_heads_per_step(s)
    wd = hps * hd
    nhb = B_HEADS // hps
    z0 = 2 * D_MODEL // wd
    kern = functools.partial(_gdn_kernel, seq_len=s, heads=hps)
    n_slots = 2 * 2 * GDN_GROUP
    return pl.pallas_call(
        kern,
        out_shape=jax.ShapeDtypeStruct((t, B_WIDTH), BF16),
        grid=(t // s, nhb),
        in_specs=[
            pl.BlockSpec((hps, s, hd), lambda n, h: (h, n, 0)),
            pl.BlockSpec((hps, s, hd), lambda n, h: (nhb + h, n, 0)),
            pl.BlockSpec((hps, s, hd), lambda n, h: (2 * nhb + h, n, 0)),
            pl.BlockSpec((hps, s, 128), lambda n, h: (h, n, 0)),
            pl.BlockSpec((hps, s // B_CHUNK, 2, 128), lambda n, h: (h, n, 0, 0)),
            pl.BlockSpec((s, wd), lambda n, h: (n, z0 + h)),
            pl.BlockSpec((1, hd), lambda n, h: (0, 0)),
        ],
        out_specs=pl.BlockSpec((s, wd), lambda n, h: (n, h)),
        scratch_shapes=[pltpu.VMEM((hps * s, hd), F32),
                        pltpu.VMEM((n_slots, hd + B_CHUNK, hd), BF16),
                        pltpu.VMEM((n_slots, hd, hd), F32),
                        pltpu.VMEM((n_slots, B_CHUNK, hd), F32),
                        pltpu.VMEM((n_slots, 8, hd), F32)],
        compiler_params=pltpu.CompilerParams(
            dimension_semantics=("parallel", "parallel"), vmem_limit_bytes=VMEM_LIMIT),
        name="gdn",
    )(qkv, qkv, qkv, gbh, grh, main, gout)


def _merge_kernel(u_ref, v_ref, ws_ref, bst_ref, b_ref, ga_ref, gb_ref, x_ref,
                  wa_ref, wb_ref, wo_ref, gffn_ref, x1_ref, h2_ref, a_s):
    tm = u_ref.shape[0]
    gd = A_WIDTH // A_GROUPS
    for ci in range(tm // A_CHUNK):
        rows = slice(ci * A_CHUNK, (ci + 1) * A_CHUNK)
        for g in range(A_GROUPS):
            cols = slice(g * gd, (g + 1) * gd)
            mixed = _dot(ws_ref[g], v_ref[rows, cols]) + bst_ref[:, g:g + 1]
            a_s[rows, cols] = (u_ref[rows, cols].astype(F32) * mixed).astype(BF16)
    pa = _dot(a_s[...], wa_ref[...])
    pb = _dot(b_ref[...], wb_ref[...])
    merged = ga_ref[...].astype(F32) * pa + gb_ref[...].astype(F32) * pb
    x1 = x_ref[...] + _dot(merged.astype(BF16), wo_ref[...])
    x1_ref[...] = x1
    h2_ref[...] = _rms(x1, gffn_ref[...]).astype(BF16)


def _merge(uv, main, b, x, ws, bst, wa, wb, wo, gffn):
    t = x.shape[0]
    tm = TOKEN_TILE
    gate0 = 0
    const = dict(pipeline_mode=pl.Buffered(1))
    return pl.pallas_call(
        _merge_kernel,
        out_shape=(jax.ShapeDtypeStruct((t, D_MODEL), F32),
                   jax.ShapeDtypeStruct((t, D_MODEL), BF16)),
        grid=(t // tm,),
        in_specs=[
            pl.BlockSpec((tm, A_WIDTH), lambda i: (i, 0)),
            pl.BlockSpec((tm, A_WIDTH), lambda i: (i, 1)),
            pl.BlockSpec((A_GROUPS, A_CHUNK, A_CHUNK), lambda i: (0, 0, 0), **const),
            pl.BlockSpec((A_CHUNK, A_GROUPS), lambda i: (0, 0), **const),
            pl.BlockSpec((tm, B_WIDTH), lambda i: (i, 0)),
            pl.BlockSpec((tm, D_MODEL), lambda i: (i, gate0)),
            pl.BlockSpec((tm, D_MODEL), lambda i: (i, gate0 + 1)),
            pl.BlockSpec((tm, D_MODEL), lambda i: (i, 0)),
            pl.BlockSpec((A_WIDTH, D_MODEL), lambda i: (0, 0), **const),
            pl.BlockSpec((B_WIDTH, D_MODEL), lambda i: (0, 0), **const),
            pl.BlockSpec((D_MODEL, D_MODEL), lambda i: (0, 0), **const),
            pl.BlockSpec((1, D_MODEL), lambda i: (0, 0), **const),
        ],
        out_specs=(
            pl.BlockSpec((tm, D_MODEL), lambda i: (i, 0)),
            pl.BlockSpec((tm, D_MODEL), lambda i: (i, 0)),
        ),
        scratch_shapes=[pltpu.VMEM((tm, A_WIDTH), BF16)],
        compiler_params=pltpu.CompilerParams(
            dimension_semantics=("parallel",), vmem_limit_bytes=VMEM_LIMIT),
        name="merge",
    )(uv, uv, ws, bst, b, main, main, x, wa, wb, wo, gffn)


def _ffn_kernel(h_ref, hp_ref, hn_ref, x1_ref, w1_ref, w2_ref, c1_ref, c2_ref, wd_ref,
                x2_ref, hc, acc, *, tiles_per_seq):
    i = pl.program_id(0)
    j = pl.program_id(1)
    tm = h_ref.shape[0]

    @pl.when(j == 0)
    def _():
        _halo_rows(hc, h_ref, hp_ref, hn_ref, i, tiles_per_seq)
        acc[...] = jnp.zeros_like(acc)

    lhs = hc[...]
    a1 = _conv_rows(_dot(lhs, w1_ref[...]), c1_ref[...], tm)
    a2 = _conv_rows(_dot(lhs, w2_ref[...]), c2_ref[...], tm)
    act = (a1 * jax.nn.sigmoid(a1) * a2).astype(BF16)
    acc[...] += _dot(act, wd_ref[...])

    @pl.when(j == pl.num_programs(1) - 1)
    def _():
        x2_ref[...] = x1_ref[...] + acc[...]


def _ffn(h2, x1, w_up, conv_ffn, w_down, seq_len):
    t = h2.shape[0]
    tm, tf = TOKEN_TILE, FFN_COL_TILE
    nj = D_FF // tf
    prev_spec, next_spec = _halo_specs(tm, t)
    kern = functools.partial(_ffn_kernel, tiles_per_seq=seq_len // tm)
    return pl.pallas_call(
        kern,
        out_shape=jax.ShapeDtypeStruct((t, D_MODEL), F32),
        grid=(t // tm, nj),
        in_specs=[
            pl.BlockSpec((tm, D_MODEL), lambda i, j: (i, 0)),
            prev_spec,
            next_spec,
            pl.BlockSpec((tm, D_MODEL), lambda i, j: (i, 0)),
            pl.BlockSpec((D_MODEL, tf), lambda i, j: (0, j)),
            pl.BlockSpec((D_MODEL, tf), lambda i, j: (0, nj + j)),
            pl.BlockSpec((3, tf), lambda i, j: (0, j)),
            pl.BlockSpec((3, tf), lambda i, j: (0, nj + j)),
            pl.BlockSpec((tf, D_MODEL), lambda i, j: (j, 0)),
        ],
        out_specs=pl.BlockSpec((tm, D_MODEL), lambda i, j: (i, 0)),
        scratch_shapes=[pltpu.VMEM((tm + 2 * HALO, D_MODEL), BF16),
                        pltpu.VMEM((tm, D_MODEL), F32)],
        compiler_params=pltpu.CompilerParams(
            dimension_semantics=("parallel", "arbitrary"), vmem_limit_bytes=VMEM_LIMIT),
        name="ffn",
    )(h2, h2, h2, x1, w_up, w_up, conv_ffn, conv_ffn, w_down)


def _ple_final_kernel(x2_ref, p_ref, gple_ref, wg_ref, wp_ref, gfin_ref, y_ref, *, final):
    x2 = x2_ref[...]
    gate = jax.nn.sigmoid(_dot(_rms(x2, gple_ref[...]).astype(BF16), wg_ref[...]))
    x3 = x2 + gate * _dot(p_ref[...].astype(BF16), wp_ref[...])
    y_ref[...] = _rms(x3, gfin_ref[...]) if final else x3


def _ple_final(x2, p, gple, wg, wp, gfin, final):
    t = x2.shape[0]
    tm = TOKEN_TILE
    const = dict(pipeline_mode=pl.Buffered(1))
    return pl.pallas_call(
        functools.partial(_ple_final_kernel, final=final),
        out_shape=jax.ShapeDtypeStruct((t, D_MODEL), F32),
        grid=(t // tm,),
        in_specs=[
            pl.BlockSpec((tm, D_MODEL), lambda i: (i, 0)),
            pl.BlockSpec((tm, PLE_DIM), lambda i: (i, 0)),
            pl.BlockSpec((1, D_MODEL), lambda i: (0, 0), **const),
            pl.BlockSpec((D_MODEL, D_MODEL), lambda i: (0, 0), **const),
            pl.BlockSpec((PLE_DIM, D_MODEL), lambda i: (0, 0), **const),
            pl.BlockSpec((1, D_MODEL), lambda i: (0, 0), **const),
        ],
        out_specs=pl.BlockSpec((tm, D_MODEL), lambda i: (i, 0)),
        compiler_params=pltpu.CompilerParams(
            dimension_semantics=("parallel",), vmem_limit_bytes=VMEM_LIMIT),
        name="ple_final",
    )(x2, p, gple, wg, wp, gfin)


def _prep_layer(g_mix, w_in, g_a_v, w_s, b_s, conv_qkv, a_log_f, a_log_b, dt_bias_f, dt_bias_b,
                g_b_out, w_a_out, w_b_out, w_out, g_ffn, w_up, conv_ffn, w_down, g_ple,
                w_ple_gate, w_ple):
    n0 = 2 * A_WIDTH + 4 * B_WIDTH
    w_uv = w_in[:, :2 * A_WIDTH].astype(BF16)
    w_qkv = w_in[:, 2 * A_WIDTH:2 * A_WIDTH + 3 * B_WIDTH].astype(BF16)
    w_main = jnp.concatenate([w_in[:, n0 + N_SMALL:], w_in[:, n0 - B_WIDTH:n0]], axis=1).astype(BF16)
    sm = w_in[:, n0:n0 + N_SMALL].reshape(D_MODEL, 4, B_HEADS)
    pad = ((0, 0), (0, 128 - N_SMALL))
    per_head = jnp.stack([sm[:, 2], sm[:, 3], sm[:, 0], sm[:, 1]], axis=-1)
    w_small = jnp.pad(per_head.reshape(D_MODEL, N_SMALL), pad).astype(BF16)
    w_small_t = jnp.stack([sm[:, 2], sm[:, 3]], axis=-1).reshape(D_MODEL, 2 * B_HEADS).T.astype(BF16)
    zeros = jnp.zeros((B_HEADS,), F32)
    alog_r = jnp.pad(jnp.stack([a_log_f, a_log_b, zeros, zeros], axis=-1).reshape(1, N_SMALL), pad)
    dtb_r = jnp.pad(jnp.stack([dt_bias_f, dt_bias_b, zeros, zeros], axis=-1).reshape(1, N_SMALL), pad)
    alog_c = jnp.stack([a_log_f, a_log_b], axis=-1).reshape(2 * B_HEADS, 1)
    dtb_c = jnp.stack([dt_bias_f, dt_bias_b], axis=-1).reshape(2 * B_HEADS, 1)
    return dict(
        g_mix=g_mix.reshape(1, D_MODEL), w_uv=w_uv, w_qkv=w_qkv, w_main=w_main, w_small=w_small,
        w_small_t=w_small_t,
        g_a_v=g_a_v.reshape(1, A_WIDTH), w_s=w_s.astype(BF16), bst=b_s.T,
        conv_qkv=conv_qkv, alog_r=alog_r, dtb_r=dtb_r, alog_c=alog_c, dtb_c=dtb_c,
        g_b_out=g_b_out.reshape(1, B_HEAD_DIM), w_a_out=w_a_out.astype(BF16),
        w_b_out=w_b_out.astype(BF16), w_out=w_out.astype(BF16), g_ffn=g_ffn.reshape(1, D_MODEL),
        w_up=w_up.astype(BF16), conv_ffn=conv_ffn, w_down=w_down.astype(BF16),
        g_ple=g_ple.reshape(1, D_MODEL), w_ple_gate=w_ple_gate.astype(BF16),
        w_ple=w_ple.astype(BF16))


def _layer(x, p, w, gfin, seq_len, final):
    h, uv, small, small_t = _in_proj_a(x, w["g_mix"], w["w_uv"], w["w_small"], w["w_small_t"],
                                       w["g_a_v"])
    qkv = _in_proj_qkv(h, w["w_qkv"], w["conv_qkv"], seq_len)
    main = _in_proj_b(h, w["w_main"])
    gbh, grh = _gate_prep(small, small_t, w["alog_r"], w["dtb_r"], w["alog_c"], w["dtb_c"])
    b = _gdn(qkv, main, gbh, grh, w["g_b_out"], seq_len)
    x1, h2 = _merge(uv, main, b, x, w["w_s"], w["bst"], w["w_a_out"], w["w_b_out"], w["w_out"],
                    w["g_ffn"])
    x2 = _ffn(h2, x1, w["w_up"], w["conv_ffn"], w["w_down"], seq_len)
    return _ple_final(x2, p, w["g_ple"], w["w_ple_gate"], w["w_ple"], gfin, final)


def kernel(x_prompt, x_sample, p_prompt, p_sample, g_mix, w_in, g_a_v, w_s, b_s, conv_qkv, a_log_f, a_log_b, dt_bias_f, dt_bias_b, g_b_out, w_a_out, w_b_out, w_out, g_ffn, w_up, conv_ffn, w_down, g_ple, w_ple_gate, w_ple, g_final):
    depth = w_in.shape[0]
    per_layer = (g_mix, w_in, g_a_v, w_s, b_s, conv_qkv, a_log_f, a_log_b, dt_bias_f, dt_bias_b,
                 g_b_out, w_a_out, w_b_out, w_out, g_ffn, w_up, conv_ffn, w_down, g_ple,
                 w_ple_gate, w_ple)
    layers = [_prep_layer(*(a[i] for a in per_layer)) for i in range(depth)]
    gfin = g_final.reshape(1, D_MODEL)

    def trunk(x, p):
        bsz, seq, _ = x.shape
        xf = x.reshape(bsz * seq, D_MODEL)
        for i, w in enumerate(layers):
            xf = _layer(xf, p[i].reshape(bsz * seq, PLE_DIM), w, gfin, seq, i == depth - 1)
        return xf.reshape(bsz, seq, D_MODEL)

    return (trunk(x_prompt, p_prompt), trunk(x_sample, p_sample))
```

```python
import functools
import math

import jax
import jax.numpy as jnp
from jax import lax
from jax.experimental import pallas as pl
from jax.experimental.pallas import tpu as pltpu

F32 = jnp.float32
BF16 = jnp.bfloat16

D_MODEL = 2048
PLE_DIM = 256
A_WIDTH = 1024
A_GROUPS = 8
A_CHUNK = 128
B_HEADS = 8
B_HEAD_DIM = 128
B_WIDTH = B_HEADS * B_HEAD_DIM
B_CHUNK = 64
Q_SCALE = 1.0 / math.sqrt(B_HEAD_DIM)
D_FF = 4096
EPS = 1e-6

N_MAIN = 2 * D_MODEL + B_WIDTH
N_SMALL = 4 * B_HEADS

TOKEN_TILE = 512
PROJ_ROW_TILE = 1024
PROJ_COL_TILE = 1024
N_GATE_TILES = 2 * D_MODEL // PROJ_COL_TILE
FFN_COL_TILE = 1024
HALO = 16
GDN_SLAB = 256
GDN_GROUP = 8
VMEM_LIMIT = 60000 * 1024


def _rms(x, g):
    return x * lax.rsqrt(jnp.mean(x * x, axis=-1, keepdims=True) + EPS) * g


def _gelu(x):
    return 0.5 * x * (1.0 + lax.erf(x * math.sqrt(0.5)))


def _dot(a, b):
    return jnp.dot(a, b, preferred_element_type=F32)


def _dot_nt(a, b):
    return lax.dot_general(a, b, (((1,), (1,)), ((), ())), preferred_element_type=F32)


def _dot_tn(a, b):
    return lax.dot_general(a, b, (((0,), (0,)), ((), ())), preferred_element_type=F32)


def _split3(x):
    hi = x.astype(BF16)
    r1 = x - hi.astype(F32)
    mid = r1.astype(BF16)
    lo = (r1 - mid.astype(F32)).astype(BF16)
    return hi, mid, lo


def _in_proj_a_kernel(x_ref, gmix_ref, w_ref, wsm_ref, wsmt_ref, gav_ref,
                      h_ref, uv_ref, small_ref, smallt_ref):
    h = _rms(x_ref[...], gmix_ref[...]).astype(BF16)
    h_ref[...] = h
    small_ref[...] = _dot(h, wsm_ref[...])
    smallt_ref[...] = _dot_nt(wsmt_ref[...], h)
    uv_ref[:, :A_WIDTH] = _gelu(_dot(h, w_ref[:, :A_WIDTH])).astype(BF16)
    v = _gelu(_dot(h, w_ref[:, A_WIDTH:]))
    uv_ref[:, A_WIDTH:] = _rms(v, gav_ref[...]).astype(BF16)


def _in_proj_a(x, gmix, w_uv, w_small, w_small_t, gav):
    t = x.shape[0]
    tm = TOKEN_TILE
    const = dict(pipeline_mode=pl.Buffered(1))
    return pl.pallas_call(
        _in_proj_a_kernel,
        out_shape=(jax.ShapeDtypeStruct((t, D_MODEL), BF16),
                   jax.ShapeDtypeStruct((t, 2 * A_WIDTH), BF16),
                   jax.ShapeDtypeStruct((t, 128), F32),
                   jax.ShapeDtypeStruct((2 * B_HEADS, t), F32)),
        grid=(t // tm,),
        in_specs=[
            pl.BlockSpec((tm, D_MODEL), lambda i: (i, 0)),
            pl.BlockSpec((1, D_MODEL), lambda i: (0, 0), **const),
            pl.BlockSpec((D_MODEL, 2 * A_WIDTH), lambda i: (0, 0), **const),
            pl.BlockSpec((D_MODEL, 128), lambda i: (0, 0), **const),
            pl.BlockSpec((2 * B_HEADS, D_MODEL), lambda i: (0, 0), **const),
            pl.BlockSpec((1, A_WIDTH), lambda i: (0, 0), **const),
        ],
        out_specs=(
            pl.BlockSpec((tm, D_MODEL), lambda i: (i, 0)),
            pl.BlockSpec((tm, 2 * A_WIDTH), lambda i: (i, 0)),
            pl.BlockSpec((tm, 128), lambda i: (i, 0)),
            pl.BlockSpec((2 * B_HEADS, tm), lambda i: (0, i)),
        ),
        compiler_params=pltpu.CompilerParams(
            dimension_semantics=("parallel",), vmem_limit_bytes=VMEM_LIMIT),
        name="in_proj_a",
    )(x, gmix, w_uv, w_small, w_small_t, gav)


def _in_proj_b_kernel(h_ref, w_ref, out_ref):
    j = pl.program_id(1)
    acc = _dot(h_ref[...], w_ref[...])
    out_ref[...] = (jax.nn.sigmoid(acc) * jnp.where(j == N_GATE_TILES, acc, 1.0)).astype(BF16)


def _halo_rows(hc, h_ref, hp_ref, hn_ref, i, tiles_per_seq):
    tm = h_ref.shape[0]
    at_start = (i % tiles_per_seq) == 0
    at_end = (i % tiles_per_seq) == tiles_per_seq - 1
    hc[0:HALO, :] = jnp.where(at_start, jnp.zeros_like(hp_ref[...]), hp_ref[...])
    hc[HALO:HALO + tm, :] = h_ref[...]
    hc[HALO + tm:, :] = jnp.where(at_end, jnp.zeros_like(hn_ref[...]), hn_ref[...])


def _conv_rows(up, cw, tm):
    n_rows = tm + 2 * HALO
    prev = pltpu.roll(up, 1, 0)[HALO:HALO + tm]
    nxt = pltpu.roll(up, n_rows - 1, 0)[HALO:HALO + tm]
    return cw[0:1, :] * prev + cw[1:2, :] * up[HALO:HALO + tm] + cw[2:3, :] * nxt


def _halo_specs(tm, t):
    hb = tm // HALO
    last = t // HALO - 1
    return (pl.BlockSpec((HALO, D_MODEL), lambda i, j: (jnp.maximum(i * hb - 1, 0), 0)),
            pl.BlockSpec((HALO, D_MODEL), lambda i, j: (jnp.minimum((i + 1) * hb, last), 0)))


def _in_proj_qkv_kernel(h_ref, hp_ref, hn_ref, w_ref, cw_ref, out_ref, hc, *, tiles_per_seq):
    i = pl.program_id(0)
    j = pl.program_id(1)
    tm = h_ref.shape[0]

    @pl.when(j == 0)
    def _():
        _halo_rows(hc, h_ref, hp_ref, hn_ref, i, tiles_per_seq)

    y = _conv_rows(_dot(hc[...], w_ref[...]), cw_ref[...], tm)
    y = y * jax.nn.sigmoid(y)
    scale = jnp.where(j == 0, Q_SCALE, 1.0)
    for g in range(B_HEADS):
        yg = y[:, g * B_HEAD_DIM:(g + 1) * B_HEAD_DIM]
        inv = lax.rsqrt(jnp.sum(yg * yg, axis=-1, keepdims=True) + EPS) * scale
        out_ref[g] = (yg * jnp.where(j < 2, inv, 1.0)).astype(BF16)


def _in_proj_qkv(h, w_qkv, conv_qkv, seq_len):
    t = h.shape[0]
    tm = PROJ_ROW_TILE
    prev_spec, next_spec = _halo_specs(tm, t)
    return pl.pallas_call(
        functools.partial(_in_proj_qkv_kernel, tiles_per_seq=seq_len // tm),
        out_shape=jax.ShapeDtypeStruct((3 * B_HEADS, t, B_HEAD_DIM), BF16),
        grid=(t // tm, 3),
        in_specs=[
            pl.BlockSpec((tm, D_MODEL), lambda i, j: (i, 0)),
            prev_spec,
            next_spec,
            pl.BlockSpec((D_MODEL, B_WIDTH), lambda i, j: (0, j)),
            pl.BlockSpec((3, B_WIDTH), lambda i, j: (0, j)),
        ],
        out_specs=pl.BlockSpec((B_HEADS, tm, B_HEAD_DIM), lambda i, j: (j, i, 0)),
        scratch_shapes=[pltpu.VMEM((tm + 2 * HALO, D_MODEL), BF16)],
        compiler_params=pltpu.CompilerParams(
            dimension_semantics=("parallel", "arbitrary"), vmem_limit_bytes=VMEM_LIMIT),
        name="in_proj_qkv",
    )(h, h, h, w_qkv, conv_qkv)


def _in_proj_b(h, w_main):
    t = h.shape[0]
    tm, tn = PROJ_ROW_TILE, PROJ_COL_TILE
    return pl.pallas_call(
        _in_proj_b_kernel,
        out_shape=jax.ShapeDtypeStruct((t, N_MAIN), BF16),
        grid=(t // tm, N_MAIN // tn),
        in_specs=[
            pl.BlockSpec((tm, D_MODEL), lambda i, j: (i, 0)),
            pl.BlockSpec((D_MODEL, tn), lambda i, j: (0, j)),
        ],
        out_specs=pl.BlockSpec((tm, tn), lambda i, j: (i, j)),
        compiler_params=pltpu.CompilerParams(
            dimension_semantics=("parallel", "arbitrary"), vmem_limit_bytes=VMEM_LIMIT),
        name="in_proj_b",
    )(h, w_main)


def _softplus(x):
    return jnp.maximum(x, 0.0) + jnp.log1p(jnp.exp(-jnp.abs(x)))


def _gate_prep_kernel(small_ref, smallt_ref, alog_r, dtb_r, alog_c, dtb_c, gbh_ref, grh_ref):
    tb, nl = small_ref.shape
    row = lax.broadcasted_iota(jnp.int32, (tb, tb), 0)
    col = lax.broadcasted_iota(jnp.int32, (tb, tb), 1)
    same = (row // B_CHUNK) == (col // B_CHUNK)
    m_lo = jnp.where(same & (col <= row), 1.0, 0.0).astype(BF16)
    m_up = jnp.where(same & (col >= row), 1.0, 0.0).astype(BF16)

    sm = small_ref[...]
    g = -jnp.exp(alog_r[...]) * _softplus(sm + dtb_r[...])
    g3 = jnp.concatenate(_split3(g), axis=1)
    pf = _dot(m_lo, g3)
    pb = _dot(m_up, g3)
    pf = pf[:, 0:nl] + pf[:, nl:2 * nl] + pf[:, 2 * nl:3 * nl]
    pb = pb[:, 0:nl] + pb[:, nl:2 * nl] + pb[:, 2 * nl:3 * nl]
    lane = lax.broadcasted_iota(jnp.int32, (tb, nl), 1) % 4
    vals = jnp.where(lane == 0, pf, jnp.where(lane == 1, pb, jax.nn.sigmoid(sm)))
    for h in range(B_HEADS):
        gbh_ref[h] = vals if h == 0 else pltpu.roll(vals, nl - 4 * h, 1)

    gt = -jnp.exp(alog_c[...]) * _softplus(smallt_ref[...] + dtb_c[...])
    gt3 = jnp.concatenate(_split3(gt), axis=0)
    nr = 2 * B_HEADS
    rf = _dot(gt3, m_up)
    rb = _dot(gt3, m_lo)
    rf = rf[0:nr] + rf[nr:2 * nr] + rf[2 * nr:3 * nr]
    rb = rb[0:nr] + rb[nr:2 * nr] + rb[2 * nr:3 * nr]
    sub = lax.broadcasted_iota(jnp.int32, (nr, tb), 0) % 2
    grow = jnp.where(sub == 0, rf, rb)
    for c in range(tb // B_CHUNK):
        for h in range(B_HEADS):
            grh_ref[h, c, :, 0:B_CHUNK] = grow[2 * h:2 * h + 2, c * B_CHUNK:(c + 1) * B_CHUNK]
            grh_ref[h, c, :, B_CHUNK:] = jnp.zeros((2, 128 - B_CHUNK), F32)


def _gate_prep(small, small_t, alog_r, dtb_r, alog_c, dtb_c):
    t = small.shape[0]
    tb = TOKEN_TILE
    nc = tb // B_CHUNK
    return pl.pallas_call(
        _gate_prep_kernel,
        out_shape=(jax.ShapeDtypeStruct((B_HEADS, t, 128), F32),
                   jax.ShapeDtypeStruct((B_HEADS, t // B_CHUNK, 2, 128), F32)),
        grid=(t // tb,),
        in_specs=[
            pl.BlockSpec((tb, 128), lambda i: (i, 0)),
            pl.BlockSpec((2 * B_HEADS, tb), lambda i: (0, i)),
            pl.BlockSpec((1, 128), lambda i: (0, 0)),
            pl.BlockSpec((1, 128), lambda i: (0, 0)),
            pl.BlockSpec((2 * B_HEADS, 1), lambda i: (0, 0)),
            pl.BlockSpec((2 * B_HEADS, 1), lambda i: (0, 0)),
        ],
        out_specs=(
            pl.BlockSpec((B_HEADS, tb, 128), lambda i: (0, i, 0)),
            pl.BlockSpec((B_HEADS, nc, 2, 128), lambda i: (0, i, 0, 0)),
        ),
        compiler_params=pltpu.CompilerParams(
            dimension_semantics=("parallel",), vmem_limit_bytes=VMEM_LIMIT),
        name="gate_prep",
    )(small, small_t, alog_r, dtb_r, alog_c, dtb_c)


def _gdn_kernel(q_ref, k_ref, v_ref, gbh_ref, grh_ref, z_ref, gout_ref,
                o_ref, oacc, lhs_s, b_s, ob_s, egl_s, *, seq_len, heads):
    s = seq_len
    c = B_CHUNK
    n_chunks = s // c
    sl = min(GDN_SLAB, s)
    oacc[...] = jnp.zeros_like(oacc)

    rowi = lax.broadcasted_iota(jnp.int32, (c, 2 * c), 0)
    coli = lax.broadcasted_iota(jnp.int32, (c, 2 * c), 1)
    zk = jnp.zeros((c, B_HEAD_DIM), BF16)
    zw = jnp.zeros((c, 2 * B_HEAD_DIM), BF16)
    zw2 = jnp.zeros((c, 4 * c), BF16)

    grp = GDN_GROUP
    n_groups = n_chunks // grp
    per_slot = 2 * grp
    n_stages = 4 + int(math.log2(c))

    def group_chunks(gi):
        hh = gi // n_groups
        gl_ = gi % n_groups
        out = []
        for g in range(grp):
            out += [(hh, gl_ * grp + g, True), (hh, n_chunks - 1 - (gl_ * grp + g), False)]
        return out

    def prep_stages(gi, slot):
        st = []
        for hh, ci, fwd in group_chunks(gi):
            r0 = pl.multiple_of(ci * c, c)
            q = q_ref[hh, pl.ds(r0, c), :]
            k = k_ref[hh, pl.ds(r0, c), :]
            v = v_ref[hh, pl.ds(r0, c), :]
            gb = gbh_ref[hh, pl.ds(r0, c), :]
            gr = grh_ref[hh, ci]
            if fwd:
                gi_, beta, gj = gb[:, 0:1], gb[:, 2:3], gr[0:1, :]
                gl = gb[c - 1:c, 0:1]
                inc, strict = rowi >= coli, rowi > coli
            else:
                gi_, beta, gj = gb[:, 1:2], gb[:, 3:4], gr[1:2, :]
                gl = gb[0:1, 1:2]
                inc, strict = (rowi <= coli) & (coli < c), (rowi < coli) & (coli < c)
            kf = k.astype(F32)
            kb = kf * beta
            st.append(dict(q=q, k=k, v=v, gi=gi_, beta=beta, gl=gl, inc=inc, strict=strict,
                           kf=kf, kb=kb, eg=jnp.exp(gi_),
                           dec=jnp.exp(jnp.where(inc, gi_ - gj, -1e30))))
        for d in st:
            d["a"] = _dot_nt(jnp.concatenate([d["kb"].astype(BF16), d["q"]], axis=0),
                             jnp.concatenate([d["k"], zk], axis=0))
        yield
        for d in st:
            l_pad = jnp.where(d["strict"], d["a"][0:c] * d["dec"], 0.0)
            d["qk"] = jnp.where(d["inc"], d["a"][c:2 * c] * d["dec"], 0.0).astype(BF16)
            d["px"] = jnp.where(coli == rowi + c, 1.0, 0.0) - l_pad
        for _ in range(int(math.log2(c))):
            for d in st:
                hi = d["px"].astype(BF16)
                lo = (d["px"] - hi.astype(F32)).astype(BF16)
                rhs = jnp.concatenate([jnp.concatenate([hi, lo], axis=1), zw2], axis=0)
                d["o"] = _dot(hi, rhs)
                d["o2"] = _dot(lo, jnp.concatenate([hi, zk], axis=0))
            yield
            for d in st:
                o = d["o"]
                d["px"] = (o[:, :2 * c] + o[:, 2 * c:] + d["o2"]
                           + jnp.where(coli >= c, d["px"], 0.0))
        for d in st:
            rhs = jnp.concatenate([d["kb"] * d["eg"], d["v"].astype(F32) * d["beta"]], axis=1)
            d["wu"] = _dot(d["px"].astype(BF16),
                           jnp.concatenate([zw, rhs.astype(BF16)], axis=0)).astype(BF16)
        yield
        for d in st:
            kd = (d["kf"] * jnp.exp(d["gl"] - d["gi"])).astype(BF16)
            d["m1"] = _dot_tn(kd, d["wu"])
        yield
        for d in st:
            d["m2"] = _dot(d["qk"], jnp.concatenate([d["wu"], zw], axis=0))
        yield
        for j, d in enumerate(st):
            idx = slot * per_slot + j
            qe = d["q"].astype(F32) * d["eg"] - d["m2"][:, :B_HEAD_DIM]
            lhs_s[idx] = jnp.concatenate([-d["m1"][:, :B_HEAD_DIM], qe], axis=0).astype(BF16)
            b_s[idx] = d["m1"][:, B_HEAD_DIM:]
            ob_s[idx] = d["m2"][:, B_HEAD_DIM:]
            egl_s[idx] = jnp.broadcast_to(jnp.exp(d["gl"]), (8, B_HEAD_DIM))

    def scan_step(gi, slot, g, states):
        if g == 0:
            keep = jnp.where(gi % n_groups == 0, 0.0, 1.0)
            states = tuple(st_ * keep for st_ in states)
        new = []
        for (hh, ci, _), state, j in zip(group_chunks(gi)[2 * g:2 * g + 2], states,
                                          (2 * g, 2 * g + 1)):
            idx = slot * per_slot + j
            rows = pl.ds(pl.multiple_of(hh * s + ci * c, c), c)
            res = _dot(lhs_s[idx], state.astype(BF16))
            oacc[rows, :] += res[B_HEAD_DIM:] + ob_s[idx]
            new.append(egl_s[idx][0:1, :] * state + res[:B_HEAD_DIM] + b_s[idx])
        return tuple(new)

    for _ in prep_stages(0, 0):
        pass

    def step(i, states):
        slot = i % 2
        g = 0
        for k_stage, _ in enumerate(prep_stages(i + 1, 1 - slot)):
            while g < min(grp, k_stage * grp // n_stages + 1):
                states = scan_step(i, slot, g, states)
                g += 1
        assert g == grp
        return states

    z0 = jnp.zeros((B_HEAD_DIM, B_HEAD_DIM), F32)
    last = heads * n_groups - 1
    states = lax.fori_loop(0, last, step, (z0, z0))
    for g in range(grp):
        states = scan_step(last, last % 2, g, states)

    def post(i, carry, hh):
        r0 = pl.multiple_of(i * sl, sl)
        cols = slice(hh * B_HEAD_DIM, (hh + 1) * B_HEAD_DIM)
        y = (_rms(oacc[pl.ds(hh * s + r0, sl), :], gout_ref[...])
             * z_ref[pl.ds(r0, sl), cols].astype(F32))
        o_ref[pl.ds(r0, sl), cols] = y.astype(BF16)
        return carry

    for hh in range(heads):
        lax.fori_loop(0, s // sl, functools.partial(post, hh=hh), 0)


def _gdn_heads_per_step(seq_len):
    per_head = seq_len * B_HEAD_DIM * (
        3 * 2 * 2
        + 2 * 2 * 2
        + 4 * 2
        + 4)
    heads = B_HEADS
    while heads > 1 and heads * per_head > VMEM_LIMIT * 3 // 4:
        heads //= 2
    return heads


def _gdn(qkv, main, gbh, grh, gout, seq_len):
    t = main.shape[0]
    s = seq_len
    hd = B_HEAD_DIM
    hps = _gdn_heads_per_step(s)
    wd = hps * hd
    nhb = B_HEADS // hps
    z0 = 2 * D_MODEL // wd
    kern = functools.partial(_gdn_kernel, seq_len=s, heads=hps)
    n_slots = 2 * 2 * GDN_GROUP
    return pl.pallas_call(
        kern,
        out_shape=jax.ShapeDtypeStruct((t, B_WIDTH), BF16),
        grid=(t // s, nhb),
        in_specs=[
            pl.BlockSpec((hps, s, hd), lambda n, h: (h, n, 0)),
            pl.BlockSpec((hps, s, hd), lambda n, h: (nhb + h, n, 0)),
            pl.BlockSpec((hps, s, hd), lambda n, h: (2 * nhb + h, n, 0)),
            pl.BlockSpec((hps, s, 128), lambda n, h: (h, n, 0)),
            pl.BlockSpec((hps, s // B_CHUNK, 2, 128), lambda n, h: (h, n, 0, 0)),
            pl.BlockSpec((s, wd), lambda n, h: (n, z0 + h)),
            pl.BlockSpec((1, hd), lambda n, h: (0, 0)),
        ],
        out_specs=pl.BlockSpec((s, wd), lambda n, h: (n, h)),
        scratch_shapes=[pltpu.VMEM((hps * s, hd), F32),
                        pltpu.VMEM((n_slots, hd + B_CHUNK, hd), BF16),
                        pltpu.VMEM((n_slots, hd, hd), F32),
                        pltpu.VMEM((n_slots, B_CHUNK, hd), F32),
                        pltpu.VMEM((n_slots, 8, hd), F32)],
        compiler_params=pltpu.CompilerParams(
            dimension_semantics=("parallel", "parallel"), vmem_limit_bytes=VMEM_LIMIT),
        name="gdn",
    )(qkv, qkv, qkv, gbh, grh, main, gout)


def _merge_kernel(u_ref, v_ref, ws_ref, bst_ref, b_ref, ga_ref, gb_ref, x_ref,
                  wa_ref, wb_ref, wo_ref, gffn_ref, x1_ref, h2_ref, a_s):
    tm = u_ref.shape[0]
    gd = A_WIDTH // A_GROUPS
    for ci in range(tm // A_CHUNK):
        rows = slice(ci * A_CHUNK, (ci + 1) * A_CHUNK)
        for g in range(A_GROUPS):
            cols = slice(g * gd, (g + 1) * gd)
            mixed = _dot(ws_ref[g], v_ref[rows, cols]) + bst_ref[:, g:g + 1]
            a_s[rows, cols] = (u_ref[rows, cols].astype(F32) * mixed).astype(BF16)
    pa = _dot(a_s[...], wa_ref[...])
    pb = _dot(b_ref[...], wb_ref[...])
    merged = ga_ref[...].astype(F32) * pa + gb_ref[...].astype(F32) * pb
    x1 = x_ref[...] + _dot(merged.astype(BF16), wo_ref[...])
    x1_ref[...] = x1
    h2_ref[...] = _rms(x1, gffn_ref[...]).astype(BF16)


def _merge(uv, main, b, x, ws, bst, wa, wb, wo, gffn):
    t = x.shape[0]
    tm = TOKEN_TILE
    gate0 = 0
    const = dict(pipeline_mode=pl.Buffered(1))
    return pl.pallas_call(
        _merge_kernel,
        out_shape=(jax.ShapeDtypeStruct((t, D_MODEL), F32),
                   jax.ShapeDtypeStruct((t, D_MODEL), BF16)),
        grid=(t // tm,),
        in_specs=[
            pl.BlockSpec((tm, A_WIDTH), lambda i: (i, 0)),
            pl.BlockSpec((tm, A_WIDTH), lambda i: (i, 1)),
            pl.BlockSpec((A_GROUPS, A_CHUNK, A_CHUNK), lambda i: (0, 0, 0), **const),
            pl.BlockSpec((A_CHUNK, A_GROUPS), lambda i: (0, 0), **const),
            pl.BlockSpec((tm, B_WIDTH), lambda i: (i, 0)),
            pl.BlockSpec((tm, D_MODEL), lambda i: (i, gate0)),
            pl.BlockSpec((tm, D_MODEL), lambda i: (i, gate0 + 1)),
            pl.BlockSpec((tm, D_MODEL), lambda i: (i, 0)),
            pl.BlockSpec((A_WIDTH, D_MODEL), lambda i: (0, 0), **const),
            pl.BlockSpec((B_WIDTH, D_MODEL), lambda i: (0, 0), **const),
            pl.BlockSpec((D_MODEL, D_MODEL), lambda i: (0, 0), **const),
            pl.BlockSpec((1, D_MODEL), lambda i: (0, 0), **const),
        ],
        out_specs=(
            pl.BlockSpec((tm, D_MODEL), lambda i: (i, 0)),
            pl.BlockSpec((tm, D_MODEL), lambda i: (i, 0)),
        ),
        scratch_shapes=[pltpu.VMEM((tm, A_WIDTH), BF16)],
        compiler_params=pltpu.CompilerParams(
            dimension_semantics=("parallel",), vmem_limit_bytes=VMEM_LIMIT),
        name="merge",
    )(uv, uv, ws, bst, b, main, main, x, wa, wb, wo, gffn)


def _ffn_kernel(h_ref, hp_ref, hn_ref, x1_ref, w1_ref, w2_ref, c1_ref, c2_ref, wd_ref,
                x2_ref, hc, acc, *, tiles_per_seq):
    i = pl.program_id(0)
    j = pl.program_id(1)
    tm = h_ref.shape[0]

    @pl.when(j == 0)
    def _():
        _halo_rows(hc, h_ref, hp_ref, hn_ref, i, tiles_per_seq)
        acc[...] = jnp.zeros_like(acc)

    lhs = hc[...]
    a1 = _conv_rows(_dot(lhs, w1_ref[...]), c1_ref[...], tm)
    a2 = _conv_rows(_dot(lhs, w2_ref[...]), c2_ref[...], tm)
    act = (a1 * jax.nn.sigmoid(a1) * a2).astype(BF16)
    acc[...] += _dot(act, wd_ref[...])

    @pl.when(j == pl.num_programs(1) - 1)
    def _():
        x2_ref[...] = x1_ref[...] + acc[...]


def _ffn(h2, x1, w_up, conv_ffn, w_down, seq_len):
    t = h2.shape[0]
    tm, tf = TOKEN_TILE, FFN_COL_TILE
    nj = D_FF // tf
    prev_spec, next_spec = _halo_specs(tm, t)
    kern = functools.partial(_ffn_kernel, tiles_per_seq=seq_len // tm)
    return pl.pallas_call(
        kern,
        out_shape=jax.ShapeDtypeStruct((t, D_MODEL), F32),
        grid=(t // tm, nj),
        in_specs=[
            pl.BlockSpec((tm, D_MODEL), lambda i, j: (i, 0)),
            prev_spec,
            next_spec,
            pl.BlockSpec((tm, D_MODEL), lambda i, j: (i, 0)),
            pl.BlockSpec((D_MODEL, tf), lambda i, j: (0, j)),
            pl.BlockSpec((D_MODEL, tf), lambda i, j: (0, nj + j)),
            pl.BlockSpec((3, tf), lambda i, j: (0, j)),
            pl.BlockSpec((3, tf), lambda i, j: (0, nj + j)),
            pl.BlockSpec((tf, D_MODEL), lambda i, j: (j, 0)),
        ],
        out_specs=pl.BlockSpec((tm, D_MODEL), lambda i, j: (i, 0)),
        scratch_shapes=[pltpu.VMEM((tm + 2 * HALO, D_MODEL), BF16),
                        pltpu.VMEM((tm, D_MODEL), F32)],
        compiler_params=pltpu.CompilerParams(
            dimension_semantics=("parallel", "arbitrary"), vmem_limit_bytes=VMEM_LIMIT),
        name="ffn",
    )(h2, h2, h2, x1, w_up, w_up, conv_ffn, conv_ffn, w_down)


def _ple_final_kernel(x2_ref, p_ref, gple_ref, wg_ref, wp_ref, gfin_ref, y_ref, *, final):
    x2 = x2_ref[...]
    gate = jax.nn.sigmoid(_dot(_rms(x2, gple_ref[...]).astype(BF16), wg_ref[...]))
    x3 = x2 + gate * _dot(p_ref[...].astype(BF16), wp_ref[...])
    y_ref[...] = _rms(x3, gfin_ref[...]) if final else x3


def _ple_final(x2, p, gple, wg, wp, gfin, final):
    t = x2.shape[0]
    tm = TOKEN_TILE
    const = dict(pipeline_mode=pl.Buffered(1))
    return pl.pallas_call(
        functools.partial(_ple_final_kernel, final=final),
        out_shape=jax.ShapeDtypeStruct((t, D_MODEL), F32),
        grid=(t // tm,),
        in_specs=[
            pl.BlockSpec((tm, D_MODEL), lambda i: (i, 0)),
            pl.BlockSpec((tm, PLE_DIM), lambda i: (i, 0)),
            pl.BlockSpec((1, D_MODEL), lambda i: (0, 0), **const),
            pl.BlockSpec((D_MODEL, D_MODEL), lambda i: (0, 0), **const),
            pl.BlockSpec((PLE_DIM, D_MODEL), lambda i: (0, 0), **const),
            pl.BlockSpec((1, D_MODEL), lambda i: (0, 0), **const),
        ],
        out_specs=pl.BlockSpec((tm, D_MODEL), lambda i: (i, 0)),
        compiler_params=pltpu.CompilerParams(
            dimension_semantics=("parallel",), vmem_limit_bytes=VMEM_LIMIT),
        name="ple_final",
    )(x2, p, gple, wg, wp, gfin)


def _prep_layer(g_mix, w_in, g_a_v, w_s, b_s, conv_qkv, a_log_f, a_log_b, dt_bias_f, dt_bias_b,
                g_b_out, w_a_out, w_b_out, w_out, g_ffn, w_up, conv_ffn, w_down, g_ple,
                w_ple_gate, w_ple):
    n0 = 2 * A_WIDTH + 4 * B_WIDTH
    w_uv = w_in[:, :2 * A_WIDTH].astype(BF16)
    w_qkv = w_in[:, 2 * A_WIDTH:2 * A_WIDTH + 3 * B_WIDTH].astype(BF16)
    w_main = jnp.concatenate([w_in[:, n0 + N_SMALL:].astype(BF16),
                              w_in[:, n0 - B_WIDTH:n0].astype(BF16)], axis=1)
    sm = w_in[:, n0:n0 + N_SMALL].reshape(D_MODEL, 4, B_HEADS)
    pad = ((0, 0), (0, 128 - N_SMALL))
    per_head = jnp.stack([sm[:, 2], sm[:, 3], sm[:, 0], sm[:, 1]], axis=-1)
    w_small = jnp.pad(per_head.reshape(D_MODEL, N_SMALL), pad).astype(BF16)
    w_small_t = jnp.stack([sm[:, 2], sm[:, 3]], axis=-1).reshape(D_MODEL, 2 * B_HEADS).T.astype(BF16)
    zeros = jnp.zeros((B_HEADS,), F32)
    alog_r = jnp.pad(jnp.stack([a_log_f, a_log_b, zeros, zeros], axis=-1).reshape(1, N_SMALL), pad)
    dtb_r = jnp.pad(jnp.stack([dt_bias_f, dt_bias_b, zeros, zeros], axis=-1).reshape(1, N_SMALL), pad)
    alog_c = jnp.stack([a_log_f, a_log_b], axis=-1).reshape(2 * B_HEADS, 1)
    dtb_c = jnp.stack([dt_bias_f, dt_bias_b], axis=-1).reshape(2 * B_HEADS, 1)
    return dict(
        g_mix=g_mix.reshape(1, D_MODEL), w_uv=w_uv, w_qkv=w_qkv, w_main=w_main, w_small=w_small,
        w_small_t=w_small_t,
        g_a_v=g_a_v.reshape(1, A_WIDTH), w_s=w_s.astype(BF16), bst=b_s.T,
        conv_qkv=conv_qkv, alog_r=alog_r, dtb_r=dtb_r, alog_c=alog_c, dtb_c=dtb_c,
        g_b_out=g_b_out.reshape(1, B_HEAD_DIM), w_a_out=w_a_out.astype(BF16),
        w_b_out=w_b_out.astype(BF16), w_out=w_out.astype(BF16), g_ffn=g_ffn.reshape(1, D_MODEL),
        w_up=w_up.astype(BF16), conv_ffn=conv_ffn, w_down=w_down.astype(BF16),
        g_ple=g_ple.reshape(1, D_MODEL), w_ple_gate=w_ple_gate.astype(BF16),
        w_ple=w_ple.astype(BF16))


def _layer(x, p, w, gfin, seq_len, final):
    h, uv, small, small_t = _in_proj_a(x, w["g_mix"], w["w_uv"], w["w_small"], w["w_small_t"],
                                       w["g_a_v"])
    qkv = _in_proj_qkv(h, w["w_qkv"], w["conv_qkv"], seq_len)
    main = _in_proj_b(h, w["w_main"])
    gbh, grh = _gate_prep(small, small_t, w["alog_r"], w["dtb_r"], w["alog_c"], w["dtb_c"])
    b = _gdn(qkv, main, gbh, grh, w["g_b_out"], seq_len)
    x1, h2 = _merge(uv, main, b, x, w["w_s"], w["bst"], w["w_a_out"], w["w_b_out"], w["w_out"],
                    w["g_ffn"])
    x2 = _ffn(h2, x1, w["w_up"], w["conv_ffn"], w["w_down"], seq_len)
    return _ple_final(x2, p, w["g_ple"], w["w_ple_gate"], w["w_ple"], gfin, final)


def kernel(x_prompt, x_sample, p_prompt, p_sample, g_mix, w_in, g_a_v, w_s, b_s, conv_qkv, a_log_f, a_log_b, dt_bias_f, dt_bias_b, g_b_out, w_a_out, w_b_out, w_out, g_ffn, w_up, conv_ffn, w_down, g_ple, w_ple_gate, w_ple, g_final):
    depth = w_in.shape[0]
    per_layer = (g_mix, w_in, g_a_v, w_s, b_s, conv_qkv, a_log_f, a_log_b, dt_bias_f, dt_bias_b,
                 g_b_out, w_a_out, w_b_out, w_out, g_ffn, w_up, conv_ffn, w_down, g_ple,
                 w_ple_gate, w_ple)
    layers = [_prep_layer(*(a[i] for a in per_layer)) for i in range(depth)]
    gfin = g_final.reshape(1, D_MODEL)

    def trunk(x, p):
        bsz, seq, _ = x.shape
        xf = x.reshape(bsz * seq, D_MODEL)
        for i, w in enumerate(layers):
            xf = _layer(xf, p[i].reshape(bsz * seq, PLE_DIM), w, gfin, seq, i == depth - 1)
        return xf.reshape(bsz, seq, D_MODEL)

    return (trunk(x_prompt, p_prompt), trunk(x_sample, p_sample))
```

```python
import functools
import math

import jax
import jax.numpy as jnp
from jax import lax
from jax.experimental import pallas as pl
from jax.experimental.pallas import tpu as pltpu

F32 = jnp.float32
BF16 = jnp.bfloat16

D_MODEL = 2048
PLE_DIM = 256
A_WIDTH = 1024
A_GROUPS = 8
A_CHUNK = 128
B_HEADS = 8
B_HEAD_DIM = 128
B_WIDTH = B_HEADS * B_HEAD_DIM
B_CHUNK = 64
Q_SCALE = 1.0 / math.sqrt(B_HEAD_DIM)
D_FF = 4096
EPS = 1e-6

N_MAIN = 2 * D_MODEL + B_WIDTH
N_SMALL = 4 * B_HEADS

TOKEN_TILE = 512
PROJ_ROW_TILE = 1024
PROJ_COL_TILE = 1024
N_GATE_TILES = 2 * D_MODEL // PROJ_COL_TILE
FFN_COL_TILE = 1024
HALO = 16
GDN_SLAB = 256
GDN_GROUP = 8
VMEM_LIMIT = 60000 * 1024


def _rms(x, g):
    return x * lax.rsqrt(jnp.mean(x * x, axis=-1, keepdims=True) + EPS) * g


def _gelu(x):
    return 0.5 * x * (1.0 + lax.erf(x * math.sqrt(0.5)))


def _dot(a, b):
    return jnp.dot(a, b, preferred_element_type=F32)


def _dot_nt(a, b):
    return lax.dot_general(a, b, (((1,), (1,)), ((), ())), preferred_element_type=F32)


def _dot_tn(a, b):
    return lax.dot_general(a, b, (((0,), (0,)), ((), ())), preferred_element_type=F32)


def _split3(x):
    hi = x.astype(BF16)
    r1 = x - hi.astype(F32)
    mid = r1.astype(BF16)
    lo = (r1 - mid.astype(F32)).astype(BF16)
    return hi, mid, lo


def _in_proj_a_kernel(x_ref, gmix_ref, w_ref, wsm_ref, wsmt_ref, gav_ref,
                      h_ref, uv_ref, small_ref, smallt_ref):
    h = _rms(x_ref[...], gmix_ref[...]).astype(BF16)
    h_ref[...] = h
    small_ref[...] = _dot(h, wsm_ref[...])
    smallt_ref[...] = _dot_nt(wsmt_ref[...], h)
    uv_ref[:, :A_WIDTH] = _gelu(_dot(h, w_ref[:, :A_WIDTH])).astype(BF16)
    v = _gelu(_dot(h, w_ref[:, A_WIDTH:]))
    uv_ref[:, A_WIDTH:] = _rms(v, gav_ref[...]).astype(BF16)


def _in_proj_a(x, gmix, w_uv, w_small, w_small_t, gav):
    t = x.shape[0]
    tm = TOKEN_TILE
    const = dict(pipeline_mode=pl.Buffered(1))
    return pl.pallas_call(
        _in_proj_a_kernel,
        out_shape=(jax.ShapeDtypeStruct((t, D_MODEL), BF16),
                   jax.ShapeDtypeStruct((t, 2 * A_WIDTH), BF16),
                   jax.ShapeDtypeStruct((t, 128), F32),
                   jax.ShapeDtypeStruct((2 * B_HEADS, t), F32)),
        grid=(t // tm,),
        in_specs=[
            pl.BlockSpec((tm, D_MODEL), lambda i: (i, 0)),
            pl.BlockSpec((1, D_MODEL), lambda i: (0, 0), **const),
            pl.BlockSpec((D_MODEL, 2 * A_WIDTH), lambda i: (0, 0), **const),
            pl.BlockSpec((D_MODEL, 128), lambda i: (0, 0), **const),
            pl.BlockSpec((2 * B_HEADS, D_MODEL), lambda i: (0, 0), **const),
            pl.BlockSpec((1, A_WIDTH), lambda i: (0, 0), **const),
        ],
        out_specs=(
            pl.BlockSpec((tm, D_MODEL), lambda i: (i, 0)),
            pl.BlockSpec((tm, 2 * A_WIDTH), lambda i: (i, 0)),
            pl.BlockSpec((tm, 128), lambda i: (i, 0)),
            pl.BlockSpec((2 * B_HEADS, tm), lambda i: (0, i)),
        ),
        compiler_params=pltpu.CompilerParams(
            dimension_semantics=("parallel",), vmem_limit_bytes=VMEM_LIMIT),
        name="in_proj_a",
    )(x, gmix, w_uv, w_small, w_small_t, gav)


def _in_proj_b_kernel(h_ref, w_ref, out_ref):
    j = pl.program_id(1)
    acc = _dot(h_ref[...], w_ref[...])
    out_ref[...] = (jax.nn.sigmoid(acc) * jnp.where(j == N_GATE_TILES, acc, 1.0)).astype(BF16)


def _halo_rows(hc, h_ref, hp_ref, hn_ref, i, tiles_per_seq):
    tm = h_ref.shape[0]
    at_start = (i % tiles_per_seq) == 0
    at_end = (i % tiles_per_seq) == tiles_per_seq - 1
    hc[0:HALO, :] = jnp.where(at_start, jnp.zeros_like(hp_ref[...]), hp_ref[...])
    hc[HALO:HALO + tm, :] = h_ref[...]
    hc[HALO + tm:, :] = jnp.where(at_end, jnp.zeros_like(hn_ref[...]), hn_ref[...])


def _conv_rows(up, cw, tm):
    n_rows = tm + 2 * HALO
    prev = pltpu.roll(up, 1, 0)[HALO:HALO + tm]
    nxt = pltpu.roll(up, n_rows - 1, 0)[HALO:HALO + tm]
    return cw[0:1, :] * prev + cw[1:2, :] * up[HALO:HALO + tm] + cw[2:3, :] * nxt


def _halo_specs(tm, t):
    hb = tm // HALO
    last = t // HALO - 1
    return (pl.BlockSpec((HALO, D_MODEL), lambda i, j: (jnp.maximum(i * hb - 1, 0), 0)),
            pl.BlockSpec((HALO, D_MODEL), lambda i, j: (jnp.minimum((i + 1) * hb, last), 0)))


def _in_proj_qkv_kernel(h_ref, hp_ref, hn_ref, w_ref, cw_ref, out_ref, hc, *, tiles_per_seq):
    i = pl.program_id(0)
    j = pl.program_id(1)
    tm = h_ref.shape[0]

    @pl.when(j == 0)
    def _():
        _halo_rows(hc, h_ref, hp_ref, hn_ref, i, tiles_per_seq)

    y = _conv_rows(_dot(hc[...], w_ref[...]), cw_ref[...], tm)
    y = y * jax.nn.sigmoid(y)
    scale = jnp.where(j == 0, Q_SCALE, 1.0)
    for g in range(B_HEADS):
        yg = y[:, g * B_HEAD_DIM:(g + 1) * B_HEAD_DIM]
        inv = lax.rsqrt(jnp.sum(yg * yg, axis=-1, keepdims=True) + EPS) * scale
        out_ref[g] = (yg * jnp.where(j < 2, inv, 1.0)).astype(BF16)


def _in_proj_qkv(h, w_qkv, conv_qkv, seq_len):
    t = h.shape[0]
    tm = PROJ_ROW_TILE
    prev_spec, next_spec = _halo_specs(tm, t)
    return pl.pallas_call(
        functools.partial(_in_proj_qkv_kernel, tiles_per_seq=seq_len // tm),
        out_shape=jax.ShapeDtypeStruct((3 * B_HEADS, t, B_HEAD_DIM), BF16),
        grid=(t // tm, 3),
        in_specs=[
            pl.BlockSpec((tm, D_MODEL), lambda i, j: (i, 0)),
            prev_spec,
            next_spec,
            pl.BlockSpec((D_MODEL, B_WIDTH), lambda i, j: (0, j)),
            pl.BlockSpec((3, B_WIDTH), lambda i, j: (0, j)),
        ],
        out_specs=pl.BlockSpec((B_HEADS, tm, B_HEAD_DIM), lambda i, j: (j, i, 0)),
        scratch_shapes=[pltpu.VMEM((tm + 2 * HALO, D_MODEL), BF16)],
        compiler_params=pltpu.CompilerParams(
            dimension_semantics=("parallel", "arbitrary"), vmem_limit_bytes=VMEM_LIMIT),
        name="in_proj_qkv",
    )(h, h, h, w_qkv, conv_qkv)


def _in_proj_b(h, w_main):
    t = h.shape[0]
    tm, tn = PROJ_ROW_TILE, PROJ_COL_TILE
    return pl.pallas_call(
        _in_proj_b_kernel,
        out_shape=jax.ShapeDtypeStruct((t, N_MAIN), BF16),
        grid=(t // tm, N_MAIN // tn),
        in_specs=[
            pl.BlockSpec((tm, D_MODEL), lambda i, j: (i, 0)),
            pl.BlockSpec((D_MODEL, tn), lambda i, j: (0, j)),
        ],
        out_specs=pl.BlockSpec((tm, tn), lambda i, j: (i, j)),
        compiler_params=pltpu.CompilerParams(
            dimension_semantics=("parallel", "arbitrary"), vmem_limit_bytes=VMEM_LIMIT),
        name="in_proj_b",
    )(h, w_main)


def _softplus(x):
    return jnp.maximum(x, 0.0) + jnp.log1p(jnp.exp(-jnp.abs(x)))


def _gate_prep_kernel(small_ref, smallt_ref, alog_r, dtb_r, alog_c, dtb_c, gbh_ref, grh_ref):
    tb, nl = small_ref.shape
    row = lax.broadcasted_iota(jnp.int32, (tb, tb), 0)
    col = lax.broadcasted_iota(jnp.int32, (tb, tb), 1)
    same = (row // B_CHUNK) == (col // B_CHUNK)
    m_lo = jnp.where(same & (col <= row), 1.0, 0.0).astype(BF16)
    m_up = jnp.where(same & (col >= row), 1.0, 0.0).astype(BF16)

    sm = small_ref[...]
    g = -jnp.exp(alog_r[...]) * _softplus(sm + dtb_r[...])
    g3 = jnp.concatenate(_split3(g), axis=1)
    pf = _dot(m_lo, g3)
    pb = _dot(m_up, g3)
    pf = pf[:, 0:nl] + pf[:, nl:2 * nl] + pf[:, 2 * nl:3 * nl]
    pb = pb[:, 0:nl] + pb[:, nl:2 * nl] + pb[:, 2 * nl:3 * nl]
    lane = lax.broadcasted_iota(jnp.int32, (tb, nl), 1) % 4
    vals = jnp.where(lane == 0, pf, jnp.where(lane == 1, pb, jax.nn.sigmoid(sm)))
    for h in range(B_HEADS):
        gbh_ref[h] = vals if h == 0 else pltpu.roll(vals, nl - 4 * h, 1)

    gt = -jnp.exp(alog_c[...]) * _softplus(smallt_ref[...] + dtb_c[...])
    gt3 = jnp.concatenate(_split3(gt), axis=0)
    nr = 2 * B_HEADS
    rf = _dot(gt3, m_up)
    rb = _dot(gt3, m_lo)
    rf = rf[0:nr] + rf[nr:2 * nr] + rf[2 * nr:3 * nr]
    rb = rb[0:nr] + rb[nr:2 * nr] + rb[2 * nr:3 * nr]
    sub = lax.broadcasted_iota(jnp.int32, (nr, tb), 0) % 2
    grow = jnp.where(sub == 0, rf, rb)
    for c in range(tb // B_CHUNK):
        for h in range(B_HEADS):
            grh_ref[h, c, :, 0:B_CHUNK] = grow[2 * h:2 * h + 2, c * B_CHUNK:(c + 1) * B_CHUNK]
            grh_ref[h, c, :, B_CHUNK:] = jnp.zeros((2, 128 - B_CHUNK), F32)


def _gate_prep(small, small_t, alog_r, dtb_r, alog_c, dtb_c):
    t = small.shape[0]
    tb = TOKEN_TILE
    nc = tb // B_CHUNK
    return pl.pallas_call(
        _gate_prep_kernel,
        out_shape=(jax.ShapeDtypeStruct((B_HEADS, t, 128), F32),
                   jax.ShapeDtypeStruct((B_HEADS, t // B_CHUNK, 2, 128), F32)),
        grid=(t // tb,),
        in_specs=[
            pl.BlockSpec((tb, 128), lambda i: (i, 0)),
            pl.BlockSpec((2 * B_HEADS, tb), lambda i: (0, i)),
            pl.BlockSpec((1, 128), lambda i: (0, 0)),
            pl.BlockSpec((1, 128), lambda i: (0, 0)),
            pl.BlockSpec((2 * B_HEADS, 1), lambda i: (0, 0)),
            pl.BlockSpec((2 * B_HEADS, 1), lambda i: (0, 0)),
        ],
        out_specs=(
            pl.BlockSpec((B_HEADS, tb, 128), lambda i: (0, i, 0)),
            pl.BlockSpec((B_HEADS, nc, 2, 128), lambda i: (0, i, 0, 0)),
        ),
        compiler_params=pltpu.CompilerParams(
            dimension_semantics=("parallel",), vmem_limit_bytes=VMEM_LIMIT),
        name="gate_prep",
    )(small, small_t, alog_r, dtb_r, alog_c, dtb_c)


def _gdn_kernel(q_ref, k_ref, v_ref, gbh_ref, grh_ref, z_ref, gout_ref,
                o_ref, oacc, lhs_s, b_s, ob_s, egl_s, *, seq_len, heads):
    s = seq_len
    c = B_CHUNK
    n_chunks = s // c
    sl = min(GDN_SLAB, s)
    oacc[...] = jnp.zeros_like(oacc)

    rowi = lax.broadcasted_iota(jnp.int32, (c, 2 * c), 0)
    coli = lax.broadcasted_iota(jnp.int32, (c, 2 * c), 1)
    zk = jnp.zeros((c, B_HEAD_DIM), BF16)
    zw = jnp.zeros((c, 2 * B_HEAD_DIM), BF16)
    zw2 = jnp.zeros((c, 4 * c), BF16)

    grp = GDN_GROUP
    n_groups = n_chunks // grp
    per_slot = 2 * grp
    n_stages = 4 + int(math.log2(c))

    def group_chunks(gi):
        hh = gi // n_groups
        gl_ = gi % n_groups
        out = []
        for g in range(grp):
            out += [(hh, gl_ * grp + g, True), (hh, n_chunks - 1 - (gl_ * grp + g), False)]
        return out

    def prep_stages(gi, slot):
        st = []
        for hh, ci, fwd in group_chunks(gi):
            r0 = pl.multiple_of(ci * c, c)
            q = q_ref[hh, pl.ds(r0, c), :]
            k = k_ref[hh, pl.ds(r0, c), :]
            v = v_ref[hh, pl.ds(r0, c), :]
            gb = gbh_ref[hh, pl.ds(r0, c), :]
            gr = grh_ref[hh, ci]
            if fwd:
                gi_, beta, gj = gb[:, 0:1], gb[:, 2:3], gr[0:1, :]
                gl = gb[c - 1:c, 0:1]
                inc, strict = rowi >= coli, rowi > coli
            else:
                gi_, beta, gj = gb[:, 1:2], gb[:, 3:4], gr[1:2, :]
                gl = gb[0:1, 1:2]
                inc, strict = (rowi <= coli) & (coli < c), (rowi < coli) & (coli < c)
            kf = k.astype(F32)
            kb = kf * beta
            st.append(dict(q=q, k=k, v=v, gi=gi_, beta=beta, gl=gl, inc=inc, strict=strict,
                           kf=kf, kb=kb, eg=jnp.exp(gi_),
                           dec=jnp.exp(jnp.where(inc, gi_ - gj, -1e30))))
        for d in st:
            d["a"] = _dot_nt(jnp.concatenate([d["kb"].astype(BF16), d["q"]], axis=0),
                             jnp.concatenate([d["k"], zk], axis=0))
        yield
        for d in st:
            l_pad = jnp.where(d["strict"], d["a"][0:c] * d["dec"], 0.0)
            d["qk"] = jnp.where(d["inc"], d["a"][c:2 * c] * d["dec"], 0.0).astype(BF16)
            d["px"] = jnp.where(coli == rowi + c, 1.0, 0.0) - l_pad
        for _ in range(int(math.log2(c))):
            for d in st:
                hi = d["px"].astype(BF16)
                lo = (d["px"] - hi.astype(F32)).astype(BF16)
                rhs = jnp.concatenate([jnp.concatenate([hi, lo], axis=1), zw2], axis=0)
                d["o"] = _dot(hi, rhs)
                d["o2"] = _dot(lo, jnp.concatenate([hi, zk], axis=0))
            yield
            for d in st:
                o = d["o"]
                d["px"] = (o[:, :2 * c] + o[:, 2 * c:] + d["o2"]
                           + jnp.where(coli >= c, d["px"], 0.0))
        for d in st:
            rhs = jnp.concatenate([d["kb"] * d["eg"], d["v"].astype(F32) * d["beta"]], axis=1)
            d["wu"] = _dot(d["px"].astype(BF16),
                           jnp.concatenate([zw, rhs.astype(BF16)], axis=0)).astype(BF16)
        yield
        for d in st:
            kd = (d["kf"] * jnp.exp(d["gl"] - d["gi"])).astype(BF16)
            d["m1"] = _dot_tn(kd, d["wu"])
        yield
        for d in st:
            d["m2"] = _dot(d["qk"], jnp.concatenate([d["wu"], zw], axis=0))
        yield
        for j, d in enumerate(st):
            idx = slot * per_slot + j
            qe = d["q"].astype(F32) * d["eg"] - d["m2"][:, :B_HEAD_DIM]
            lhs_s[idx] = jnp.concatenate([-d["m1"][:, :B_HEAD_DIM], qe], axis=0).astype(BF16)
            b_s[idx] = d["m1"][:, B_HEAD_DIM:]
            ob_s[idx] = d["m2"][:, B_HEAD_DIM:]
            egl_s[idx] = jnp.broadcast_to(jnp.exp(d["gl"]), (8, B_HEAD_DIM))

    def scan_step(gi, slot, g, states):
        if g == 0:
            keep = jnp.where(gi % n_groups == 0, 0.0, 1.0)
            states = tuple(st_ * keep for st_ in states)
        new = []
        for (hh, ci, _), state, j in zip(group_chunks(gi)[2 * g:2 * g + 2], states,
                                          (2 * g, 2 * g + 1)):
            idx = slot * per_slot + j
            rows = pl.ds(pl.multiple_of(hh * s + ci * c, c), c)
            res = _dot(lhs_s[idx], state.astype(BF16))
            oacc[rows, :] += res[B_HEAD_DIM:] + ob_s[idx]
            new.append(egl_s[idx][0:1, :] * state + res[:B_HEAD_DIM] + b_s[idx])
        return tuple(new)

    for _ in prep_stages(0, 0):
        pass

    def step(i, states):
        slot = i % 2
        g = 0
        for k_stage, _ in enumerate(prep_stages(i + 1, 1 - slot)):
            while g < min(grp, k_stage * grp // n_stages + 1):
                states = scan_step(i, slot, g, states)
                g += 1
        assert g == grp
        return states

    z0 = jnp.zeros((B_HEAD_DIM, B_HEAD_DIM), F32)
    last = heads * n_groups - 1
    states = lax.fori_loop(0, last, step, (z0, z0))
    for g in range(grp):
        states = scan_step(last, last % 2, g, states)

    def post(i, carry, hh):
        r0 = pl.multiple_of(i * sl, sl)
        cols = slice(hh * B_HEAD_DIM, (hh + 1) * B_HEAD_DIM)
        y = (_rms(oacc[pl.ds(hh * s + r0, sl), :], gout_ref[...])
             * z_ref[pl.ds(r0, sl), cols].astype(F32))
        o_ref[pl.ds(r0, sl), cols] = y.astype(BF16)
        return carry

    for hh in range(heads):
        lax.fori_loop(0, s // sl, functools.partial(post, hh=hh), 0)


def _gdn_heads_per_step(seq_len):
    per_head = seq_len * B_HEAD_DIM * (
        3 * 2 * 2
        + 2 * 2 * 2
        + 4 * 2
        + 4)
    heads = B_HEADS
    while heads > 1 and heads * per_head > VMEM_LIMIT * 3 // 4:
        heads //= 2
    return heads


def _gdn(qkv, main, gbh, grh, gout, seq_len):
    t = main.shape[0]
    s = seq_len
    hd = B_HEAD_DIM
    hps = _gdn_heads_per_step(s)
    wd = hps * hd
    nhb = B_HEADS // hps
    z0 = 2 * D_MODEL // wd
    kern = functools.partial(_gdn_kernel, seq_len=s, heads=hps)
    n_slots = 2 * 2 * GDN_GROUP
    return pl.pallas_call(
        kern,
        out_shape=jax.ShapeDtypeStruct((t, B_WIDTH), BF16),
        grid=(t // s, nhb),
        in_specs=[
            pl.BlockSpec((hps, s, hd), lambda n, h: (h, n, 0)),
            pl.BlockSpec((hps, s, hd), lambda n, h: (nhb + h, n, 0)),
            pl.BlockSpec((hps, s, hd), lambda n, h: (2 * nhb + h, n, 0)),
            pl.BlockSpec((hps, s, 128), lambda n, h: (h, n, 0)),
            pl.BlockSpec((hps, s // B_CHUNK, 2, 128), lambda n, h: (h, n, 0, 0)),
            pl.BlockSpec((s, wd), lambda n, h: (n, z0 + h)),
            pl.BlockSpec((1, hd), lambda n, h: (0, 0)),
        ],
        out_specs=pl.BlockSpec((s, wd), lambda n, h: (n, h)),
        scratch_shapes=[pltpu.VMEM((hps * s, hd), F32),
                        pltpu.VMEM((n_slots, hd + B_CHUNK, hd), BF16),
                        pltpu.VMEM((n_slots, hd, hd), F32),
                        pltpu.VMEM((n_slots, B_CHUNK, hd), F32),
                        pltpu.VMEM((n_slots, 8, hd), F32)],
        compiler_params=pltpu.CompilerParams(
            dimension_semantics=("parallel", "parallel"), vmem_limit_bytes=VMEM_LIMIT),
        name="gdn",
    )(qkv, qkv, qkv, gbh, grh, main, gout)


def _merge_kernel(u_ref, v_ref, ws_ref, bst_ref, b_ref, ga_ref, gb_ref, x_ref,
                  wa_ref, wb_ref, wo_ref, gffn_ref, x1_ref, h2_ref, a_s):
    tm = u_ref.shape[0]
    gd = A_WIDTH // A_GROUPS
    for ci in range(tm // A_CHUNK):
        rows = slice(ci * A_CHUNK, (ci + 1) * A_CHUNK)
        for g in range(A_GROUPS):
            cols = slice(g * gd, (g + 1) * gd)
            mixed = _dot(ws_ref[g], v_ref[rows, cols]) + bst_ref[:, g:g + 1]
            a_s[rows, cols] = (u_ref[rows, cols].astype(F32) * mixed).astype(BF16)
    pa = _dot(a_s[...], wa_ref[...])
    pb = _dot(b_ref[...], wb_ref[...])
    merged = ga_ref[...].astype(F32) * pa + gb_ref[...].astype(F32) * pb
    x1 = x_ref[...] + _dot(merged.astype(BF16), wo_ref[...])
    x1_ref[...] = x1
    h2_ref[...] = _rms(x1, gffn_ref[...]).astype(BF16)


def _merge(uv, main, b, x, ws, bst, wa, wb, wo, gffn):
    t = x.shape[0]
    tm = TOKEN_TILE
    gate0 = 0
    const = dict(pipeline_mode=pl.Buffered(1))
    return pl.pallas_call(
        _merge_kernel,
        out_shape=(jax.ShapeDtypeStruct((t, D_MODEL), F32),
                   jax.ShapeDtypeStruct((t, D_MODEL), BF16)),
        grid=(t // tm,),
        in_specs=[
            pl.BlockSpec((tm, A_WIDTH), lambda i: (i, 0)),
            pl.BlockSpec((tm, A_WIDTH), lambda i: (i, 1)),
            pl.BlockSpec((A_GROUPS, A_CHUNK, A_CHUNK), lambda i: (0, 0, 0), **const),
            pl.BlockSpec((A_CHUNK, A_GROUPS), lambda i: (0, 0), **const),
            pl.BlockSpec((tm, B_WIDTH), lambda i: (i, 0)),
            pl.BlockSpec((tm, D_MODEL), lambda i: (i, gate0)),
            pl.BlockSpec((tm, D_MODEL), lambda i: (i, gate0 + 1)),
            pl.BlockSpec((tm, D_MODEL), lambda i: (i, 0)),
            pl.BlockSpec((A_WIDTH, D_MODEL), lambda i: (0, 0), **const),
            pl.BlockSpec((B_WIDTH, D_MODEL), lambda i: (0, 0), **const),
            pl.BlockSpec((D_MODEL, D_MODEL), lambda i: (0, 0), **const),
            pl.BlockSpec((1, D_MODEL), lambda i: (0, 0), **const),
        ],
        out_specs=(
            pl.BlockSpec((tm, D_MODEL), lambda i: (i, 0)),
            pl.BlockSpec((tm, D_MODEL), lambda i: (i, 0)),
        ),
        scratch_shapes=[pltpu.VMEM((tm, A_WIDTH), BF16)],
        compiler_params=pltpu.CompilerParams(
            dimension_semantics=("parallel",), vmem_limit_bytes=VMEM_LIMIT),
        name="merge",
    )(uv, uv, ws, bst, b, main, main, x, wa, wb, wo, gffn)


def _ffn_kernel(h_ref, hp_ref, hn_ref, x1_ref, w1_ref, w2_ref, c1_ref, c2_ref, wd_ref,
                x2_ref, hc, *, tiles_per_seq):
    i = pl.program_id(0)
    j = pl.program_id(1)
    tm = h_ref.shape[0]

    @pl.when(j == 0)
    def _():
        _halo_rows(hc, h_ref, hp_ref, hn_ref, i, tiles_per_seq)
        x2_ref[...] = x1_ref[...]

    lhs = hc[...]
    a1 = _conv_rows(_dot(lhs, w1_ref[...]), c1_ref[...], tm)
    a2 = _conv_rows(_dot(lhs, w2_ref[...]), c2_ref[...], tm)
    act = (a1 * jax.nn.sigmoid(a1) * a2).astype(BF16)
    x2_ref[...] += _dot(act, wd_ref[...])


def _ffn(h2, x1, w_up, conv_ffn, w_down, seq_len):
    t = h2.shape[0]
    tm, tf = TOKEN_TILE, FFN_COL_TILE
    nj = D_FF // tf
    prev_spec, next_spec = _halo_specs(tm, t)
    kern = functools.partial(_ffn_kernel, tiles_per_seq=seq_len // tm)
    return pl.pallas_call(
        kern,
        out_shape=jax.ShapeDtypeStruct((t, D_MODEL), F32),
        grid=(t // tm, nj),
        in_specs=[
            pl.BlockSpec((tm, D_MODEL), lambda i, j: (i, 0)),
            prev_spec,
            next_spec,
            pl.BlockSpec((tm, D_MODEL), lambda i, j: (i, 0)),
            pl.BlockSpec((D_MODEL, tf), lambda i, j: (0, j)),
            pl.BlockSpec((D_MODEL, tf), lambda i, j: (0, nj + j)),
            pl.BlockSpec((3, tf), lambda i, j: (0, j)),
            pl.BlockSpec((3, tf), lambda i, j: (0, nj + j)),
            pl.BlockSpec((tf, D_MODEL), lambda i, j: (j, 0)),
        ],
        out_specs=pl.BlockSpec((tm, D_MODEL), lambda i, j: (i, 0)),
        scratch_shapes=[pltpu.VMEM((tm + 2 * HALO, D_MODEL), BF16)],
        compiler_params=pltpu.CompilerParams(
            dimension_semantics=("parallel", "arbitrary"), vmem_limit_bytes=VMEM_LIMIT),
        name="ffn",
    )(h2, h2, h2, x1, w_up, w_up, conv_ffn, conv_ffn, w_down)


def _ple_final_kernel(x2_ref, p_ref, gple_ref, wg_ref, wp_ref, gfin_ref, y_ref, *, final):
    x2 = x2_ref[...]
    gate = jax.nn.sigmoid(_dot(_rms(x2, gple_ref[...]).astype(BF16), wg_ref[...]))
    x3 = x2 + gate * _dot(p_ref[...].astype(BF16), wp_ref[...])
    y_ref[...] = _rms(x3, gfin_ref[...]) if final else x3


def _ple_final(x2, p, gple, wg, wp, gfin, final):
    t = x2.shape[0]
    tm = TOKEN_TILE
    const = dict(pipeline_mode=pl.Buffered(1))
    return pl.pallas_call(
        functools.partial(_ple_final_kernel, final=final),
        out_shape=jax.ShapeDtypeStruct((t, D_MODEL), F32),
        grid=(t // tm,),
        in_specs=[
            pl.BlockSpec((tm, D_MODEL), lambda i: (i, 0)),
            pl.BlockSpec((tm, PLE_DIM), lambda i: (i, 0)),
            pl.BlockSpec((1, D_MODEL), lambda i: (0, 0), **const),
            pl.BlockSpec((D_MODEL, D_MODEL), lambda i: (0, 0), **const),
            pl.BlockSpec((PLE_DIM, D_MODEL), lambda i: (0, 0), **const),
            pl.BlockSpec((1, D_MODEL), lambda i: (0, 0), **const),
        ],
        out_specs=pl.BlockSpec((tm, D_MODEL), lambda i: (i, 0)),
        compiler_params=pltpu.CompilerParams(
            dimension_semantics=("parallel",), vmem_limit_bytes=VMEM_LIMIT),
        name="ple_final",
    )(x2, p, gple, wg, wp, gfin)


def _prep_layer(g_mix, w_in, g_a_v, w_s, b_s, conv_qkv, a_log_f, a_log_b, dt_bias_f, dt_bias_b,
                g_b_out, w_a_out, w_b_out, w_out, g_ffn, w_up, conv_ffn, w_down, g_ple,
                w_ple_gate, w_ple):
    n0 = 2 * A_WIDTH + 4 * B_WIDTH
    w_uv = w_in[:, :2 * A_WIDTH].astype(BF16)
    w_qkv = w_in[:, 2 * A_WIDTH:2 * A_WIDTH + 3 * B_WIDTH].astype(BF16)
    w_main = jnp.concatenate([w_in[:, n0 + N_SMALL:].astype(BF16),
                              w_in[:, n0 - B_WIDTH:n0].astype(BF16)], axis=1)
    sm = w_in[:, n0:n0 + N_SMALL].reshape(D_MODEL, 4, B_HEADS)
    pad = ((0, 0), (0, 128 - N_SMALL))
    per_head = jnp.stack([sm[:, 2], sm[:, 3], sm[:, 0], sm[:, 1]], axis=-1)
    w_small = jnp.pad(per_head.reshape(D_MODEL, N_SMALL), pad).astype(BF16)
    w_small_t = jnp.stack([sm[:, 2], sm[:, 3]], axis=-1).reshape(D_MODEL, 2 * B_HEADS).T.astype(BF16)
    zeros = jnp.zeros((B_HEADS,), F32)
    alog_r = jnp.pad(jnp.stack([a_log_f, a_log_b, zeros, zeros], axis=-1).reshape(1, N_SMALL), pad)
    dtb_r = jnp.pad(jnp.stack([dt_bias_f, dt_bias_b, zeros, zeros], axis=-1).reshape(1, N_SMALL), pad)
    alog_c = jnp.stack([a_log_f, a_log_b], axis=-1).reshape(2 * B_HEADS, 1)
    dtb_c = jnp.stack([dt_bias_f, dt_bias_b], axis=-1).reshape(2 * B_HEADS, 1)
    return dict(
        g_mix=g_mix.reshape(1, D_MODEL), w_uv=w_uv, w_qkv=w_qkv, w_main=w_main, w_small=w_small,
        w_small_t=w_small_t,
        g_a_v=g_a_v.reshape(1, A_WIDTH), w_s=w_s.astype(BF16), bst=b_s.T,
        conv_qkv=conv_qkv, alog_r=alog_r, dtb_r=dtb_r, alog_c=alog_c, dtb_c=dtb_c,
        g_b_out=g_b_out.reshape(1, B_HEAD_DIM), w_a_out=w_a_out.astype(BF16),
        w_b_out=w_b_out.astype(BF16), w_out=w_out.astype(BF16), g_ffn=g_ffn.reshape(1, D_MODEL),
        w_up=w_up.astype(BF16), conv_ffn=conv_ffn, w_down=w_down.astype(BF16),
        g_ple=g_ple.reshape(1, D_MODEL), w_ple_gate=w_ple_gate.astype(BF16),
        w_ple=w_ple.astype(BF16))


def _layer(x, p, w, gfin, seq_len, final):
    h, uv, small, small_t = _in_proj_a(x, w["g_mix"], w["w_uv"], w["w_small"], w["w_small_t"],
                                       w["g_a_v"])
    qkv = _in_proj_qkv(h, w["w_qkv"], w["conv_qkv"], seq_len)
    main = _in_proj_b(h, w["w_main"])
    gbh, grh = _gate_prep(small, small_t, w["alog_r"], w["dtb_r"], w["alog_c"], w["dtb_c"])
    b = _gdn(qkv, main, gbh, grh, w["g_b_out"], seq_len)
    x1, h2 = _merge(uv, main, b, x, w["w_s"], w["bst"], w["w_a_out"], w["w_b_out"], w["w_out"],
                    w["g_ffn"])
    x2 = _ffn(h2, x1, w["w_up"], w["conv_ffn"], w["w_down"], seq_len)
    return _ple_final(x2, p, w["g_ple"], w["w_ple_gate"], w["w_ple"], gfin, final)


def kernel(x_prompt, x_sample, p_prompt, p_sample, g_mix, w_in, g_a_v, w_s, b_s, conv_qkv, a_log_f, a_log_b, dt_bias_f, dt_bias_b, g_b_out, w_a_out, w_b_out, w_out, g_ffn, w_up, conv_ffn, w_down, g_ple, w_ple_gate, w_ple, g_final):
    depth = w_in.shape[0]
    per_layer = (g_mix, w_in, g_a_v, w_s, b_s, conv_qkv, a_log_f, a_log_b, dt_bias_f, dt_bias_b,
                 g_b_out, w_a_out, w_b_out, w_out, g_ffn, w_up, conv_ffn, w_down, g_ple,
                 w_ple_gate, w_ple)
    layers = [_prep_layer(*(a[i] for a in per_layer)) for i in range(depth)]
    gfin = g_final.reshape(1, D_MODEL)

    def trunk(x, p):
        bsz, seq, _ = x.shape
        xf = x.reshape(bsz * seq, D_MODEL)
        for i, w in enumerate(layers):
            xf = _layer(xf, p[i].reshape(bsz * seq, PLE_DIM), w, gfin, seq, i == depth - 1)
        return xf.reshape(bsz, seq, D_MODEL)

    return (trunk(x_prompt, p_prompt), trunk(x_sample, p_sample))
```
